```python
import jax
import jax.numpy as jnp
from jax import lax
import numpy as np

D_MODEL = 2048
BATCH = 2
SEQ = 4096
DEPTH = 2

F32 = jnp.float32
GRID_W = 64
CTX_LEN = 256
RMS_EPS = 1e-6
N_MOD = 6

HEAD_DIM = 128
N_HEADS = 8
N_KV_HEADS = 2
GQA_GROUP = N_HEADS // N_KV_HEADS
Q_BLOCK = 128
ROPE_THETA = 10000.0
ROPE_AXIS_DIM = HEAD_DIM // 2
ATTN_W = N_HEADS * HEAD_DIM
KV_W = N_KV_HEADS * HEAD_DIM
POOL_WINDOWS = (2, 4, 8, 16)
N_POOL_GROUPS = len(POOL_WINDOWS)
POOL_W = D_MODEL // 2
POOL_GROUP = POOL_W // N_POOL_GROUPS
AB_IN = ATTN_W + 2 * KV_W + POOL_W
AB_MIX = ATTN_W + POOL_W
RWKV_HEAD = 64
RWKV_HEADS = D_MODEL // RWKV_HEAD
DECAY_LORA = 96
AAA_LORA = 96
GATE_LORA = 256
GN_EPS = 64e-5
N_DIR = 2
N_SHIFT = 6
N_EXPERTS = 32
TOP_K = 4
D_EXPERT = D_MODEL // 2
SWIGLU_LIMIT = 7.0
SWIGLU_ALPHA = 1.702
EXPERT_BLOCK = 128

N_EVEN = (DEPTH + 1) // 2
N_ODD = DEPTH // 2

kernel_name = 'hybrid_attn_pool_rwkv7_moe_dit'


def rms_norm(x, g):
    xf = x.astype(F32)
    y = xf * lax.rsqrt(jnp.mean(jnp.square(xf), axis=-1, keepdims=True) + RMS_EPS)
    return (y * g.astype(F32)).astype(x.dtype)


def modulate(h, shift, scale):
    return h * (1.0 + scale) + shift


def grid_angles(n_tok):
    rows = n_tok // GRID_W
    row_idx = jnp.repeat(jnp.arange(rows), GRID_W).astype(F32)
    col_idx = jnp.tile(jnp.arange(GRID_W), rows).astype(F32)
    inv = ROPE_THETA ** (-jnp.arange(0, ROPE_AXIS_DIM, 2, dtype=F32) / ROPE_AXIS_DIM)
    return row_idx[:, None] * inv[None, :], col_idx[:, None] * inv[None, :]


def rope_axis(x, ang):
    x1, x2 = jnp.split(x.astype(F32), 2, axis=-1)
    cos = jnp.cos(ang)[None, :, None, :]
    sin = jnp.sin(ang)[None, :, None, :]
    return jnp.concatenate([x1 * cos - x2 * sin, x2 * cos + x1 * sin], axis=-1)


def rope_2d(x, ang_row, ang_col):
    xr, xc = jnp.split(x, 2, axis=-1)
    return jnp.concatenate([rope_axis(xr, ang_row), rope_axis(xc, ang_col)], axis=-1).astype(x.dtype)


def gqa_attend(q, k, v):
    bn, lq = q.shape[:2]
    qg = q.reshape(bn, lq, N_KV_HEADS, GQA_GROUP, HEAD_DIM).astype(F32)
    s = jnp.einsum('bqkgd,bskd->bkgqs', qg, k.astype(F32)) * (HEAD_DIM ** -0.5)
    p = jax.nn.softmax(s, axis=-1)
    o = jnp.einsum('bkgqs,bskd->bqkgd', p, v.astype(F32))
    return o.reshape(bn, lq, ATTN_W).astype(v.dtype)


def blocked_gqa_attend(q, k, v):
    bn, lq = q.shape[:2]
    nb = lq // Q_BLOCK
    qb = q.reshape(bn, nb, Q_BLOCK, N_HEADS, HEAD_DIM).transpose(1, 0, 2, 3, 4)
    ob = lax.map(lambda qi: gqa_attend(qi, k, v), qb)
    return ob.transpose(1, 0, 2, 3).reshape(bn, lq, ATTN_W)


def multiscale_pool(u, pool_w, pool_scale):
    bn, n, _ = u.shape
    ug = u.astype(F32).reshape(bn, n, N_POOL_GROUPS, POOL_GROUP)
    csum = jnp.pad(jnp.cumsum(ug, axis=1), ((0, 0), (1, 0), (0, 0), (0, 0)))
    t = jnp.arange(n)
    outs = []
    for gi, w in enumerate(POOL_WINDOWS):
        lo = jnp.clip(t - w // 2, 0, n)
        hi = jnp.clip(t + w // 2, 0, n)
        cg = csum[:, :, gi]
        win_sum = jnp.take(cg, hi, axis=1) - jnp.take(cg, lo, axis=1)
        cnt = (hi - lo).astype(F32)[None, :, None]
        outs.append(win_sum / cnt - ug[:, :, gi])
    m = jnp.stack(outs, axis=2)
    y = jnp.einsum('blgc,gcd->blgd', m, pool_w.astype(F32)) * pool_scale.astype(F32).reshape(N_POOL_GROUPS, POOL_GROUP)
    return y.reshape(bn, n, POOL_W).astype(u.dtype)


def attn_pool_mixer(h_lat, h_ctx, ang_row, ang_col, w_in, q_gain, k_gain, pool_w, pool_scale, w_out, ctx_out):
    bn, n, _ = h_lat.shape
    n_ctx = h_ctx.shape[1]
    u = h_lat @ w_in
    q, k, v, pin = jnp.split(u, [ATTN_W, ATTN_W + KV_W, ATTN_W + 2 * KV_W], axis=-1)
    q = rope_2d(rms_norm(q.reshape(bn, n, N_HEADS, HEAD_DIM), q_gain), ang_row, ang_col)
    k = rope_2d(rms_norm(k.reshape(bn, n, N_KV_HEADS, HEAD_DIM), k_gain), ang_row, ang_col)
    v = v.reshape(bn, n, N_KV_HEADS, HEAD_DIM)
    k_c, v_c = jnp.split(h_ctx @ w_in[:, ATTN_W:ATTN_W + 2 * KV_W], 2, axis=-1)
    k_c = rms_norm(k_c.reshape(bn, n_ctx, N_KV_HEADS, HEAD_DIM), k_gain)
    v_c = v_c.reshape(bn, n_ctx, N_KV_HEADS, HEAD_DIM)
    o = blocked_gqa_attend(q, jnp.concatenate([k, k_c], axis=1), jnp.concatenate([v, v_c], axis=1))
    y_lat = jnp.concatenate([o, multiscale_pool(pin, pool_w, pool_scale)], axis=-1) @ w_out
    y_ctx = None
    if ctx_out:
        q_c = rms_norm((h_ctx @ w_in[:, :ATTN_W]).reshape(bn, n_ctx, N_HEADS, HEAD_DIM), q_gain)
        pin_c = h_ctx @ w_in[:, ATTN_W + 2 * KV_W:]
        o_c = gqa_attend(q_c, k_c, v_c)
        y_ctx = jnp.concatenate([o_c, multiscale_pool(pin_c, pool_w, pool_scale)], axis=-1) @ w_out
    return y_lat, y_ctx


def centred_shift(x):
    zero = jnp.zeros_like(x[:, :1])
    prev = jnp.concatenate([zero, x[:, :-1]], axis=1)
    nxt = jnp.concatenate([x[:, 1:], zero], axis=1)
    return 0.5 * (prev + nxt) - x


def to_scan(x):
    bn, n, _ = x.shape
    return x.astype(F32).reshape(bn, n, RWKV_HEADS, RWKV_HEAD).transpose(1, 0, 2, 3)


def rwkv_prep(h, mix, wk, wv, w0, w1, w2, a0, a1, a2, k_k, k_a):
    xx = centred_shift(h)
    xr, xw, xk, xv, xa, xg = (h + xx * mix[j] for j in range(N_SHIFT))
    k = xk @ wk
    v = to_scan(xv @ wv)
    kk = to_scan(k * k_k)
    kk = kk * lax.rsqrt(jnp.maximum(jnp.sum(jnp.square(kk), axis=-1, keepdims=True), 1e-24))
    neg_kk = -kk
    terms = []
    for d in range(N_DIR):
        w_log = -jax.nn.softplus(-(w0[d] + jnp.tanh(xw @ w1[d]) @ w2[d]).astype(F32)) - 0.5
        a = jax.nn.sigmoid((a0[d] + (xa @ a1[d]) @ a2[d]).astype(F32))
        k_d = k.astype(F32) * (1.0 + (a - 1.0) * k_a.astype(F32))
        terms.append((to_scan(jnp.exp(-jnp.exp(w_log))), to_scan(k_d), v, neg_kk, kk * to_scan(a)))
    return xr, xg, terms


def _rwkv_state_update(s, w_t, k_t, v_t, a_t, b_t):
    sa = jnp.einsum('bhvk,bhk->bhv', s, a_t)
    return s * w_t[:, :, None, :] + sa[..., None] * b_t[:, :, None, :] + v_t[..., None] * k_t[:, :, None, :]


def rwkv_scan(s0, r, terms, reverse):
    def step(s, inp):
        r_t, w_t, k_t, v_t, a_t, b_t = inp
        s = _rwkv_state_update(s, w_t, k_t, v_t, a_t, b_t)
        return s, jnp.einsum('bhvk,bhk->bhv', s, r_t)
    return lax.scan(step, s0, (r,) + tuple(terms), reverse=reverse)


def rwkv_state_scan(s0, terms, reverse):
    def step(s, inp):
        return _rwkv_state_update(s, *inp), None
    s, _ = lax.scan(step, s0, tuple(terms), reverse=reverse)
    return s


def rwkv_readout(y, r, terms, xg, g1, g2, r_k, ln_w, ln_b, wo, dtype):
    n, bn = y.shape[:2]
    mu = jnp.mean(y, axis=-1, keepdims=True)
    var = jnp.mean(jnp.square(y - mu), axis=-1, keepdims=True)
    yn = (y - mu) * lax.rsqrt(var + GN_EPS)
    yn = yn * ln_w.astype(F32).reshape(RWKV_HEADS, RWKV_HEAD) + ln_b.astype(F32).reshape(RWKV_HEADS, RWKV_HEAD)
    rk = r * r_k.astype(F32)
    bonus = (jnp.sum(rk * terms[0][1], axis=-1, keepdims=True) + jnp.sum(rk * terms[1][1], axis=-1, keepdims=True)) * terms[0][2]
    o = (yn + bonus).transpose(1, 0, 2, 3).reshape(bn, n, D_MODEL)
    g = jax.nn.sigmoid(xg @ g1) @ g2
    return (o * g.astype(F32)).astype(dtype) @ wo


def rwkv_mixer(h_lat, h_ctx, mix, wr, wk, wv, wo, w0, w1, w2, a0, a1, a2, g1, g2, k_k, k_a, r_k, ln_w, ln_b, ctx_out):
    tok = (mix, wk, wv, w0, w1, w2, a0, a1, a2, k_k, k_a)
    xr_l, xg_l, terms_l = rwkv_prep(h_lat, *tok)
    xr_c, xg_c, terms_c = rwkv_prep(h_ctx, *tok)
    s0 = jnp.zeros((h_lat.shape[0], RWKV_HEADS, RWKV_HEAD, RWKV_HEAD), F32)
    r_l = to_scan(xr_l @ wr)
    r_c = to_scan(xr_c @ wr) if ctx_out else None
    ys_l, ys_c = [], []
    for d in range(N_DIR):
        rev = d == 1
        if ctx_out:
            s_c, y_c = rwkv_scan(s0, r_c, terms_c[d], rev)
            ys_c.append(y_c)
        else:
            s_c = rwkv_state_scan(s0, terms_c[d], rev)
        _, y_l = rwkv_scan(s_c, r_l, terms_l[d], rev)
        ys_l.append(y_l)
    head = (g1, g2, r_k, ln_w, ln_b, wo, h_lat.dtype)
    out_l = rwkv_readout(ys_l[0] + ys_l[1], r_l, terms_l, xg_l, *head)
    out_c = rwkv_readout(ys_c[0] + ys_c[1], r_c, terms_c, xg_c, *head) if ctx_out else None
    return out_l, out_c


def moe_ffn(h, router_w, router_b, w_gate, b_gate, w_up, b_up, w_down, b_down):
    n_tok, d = h.shape
    logits = h.astype(F32) @ router_w.astype(F32) + router_b.astype(F32)
    top_val, top_idx = lax.top_k(logits, TOP_K)
    gates = jax.nn.softmax(top_val, axis=-1)
    n_asg = n_tok * TOP_K
    flat_e = top_idx.reshape(-1)
    flat_tok = jnp.repeat(jnp.arange(n_tok), TOP_K)
    order = jnp.argsort(flat_e)
    sorted_e = flat_e[order]
    counts = jnp.bincount(flat_e, length=N_EXPERTS)
    padded = (counts + EXPERT_BLOCK - 1) // EXPERT_BLOCK * EXPERT_BLOCK
    padded_end = jnp.cumsum(padded)
    start = jnp.cumsum(counts) - counts
    dest = (padded_end - padded)[sorted_e] + jnp.arange(n_asg) - start[sorted_e]
    n_blocks = -(-(n_asg + N_EXPERTS * (EXPERT_BLOCK - 1)) // EXPERT_BLOCK)
    n_slots = n_blocks * EXPERT_BLOCK
    slot_tok = jnp.full((n_slots,), n_tok, jnp.int32).at[dest].set(flat_tok[order])
    slot_gate = jnp.zeros((n_slots,), F32).at[dest].set(gates.reshape(-1)[order])
    block_e = jnp.minimum(jnp.searchsorted(padded_end, jnp.arange(n_blocks) * EXPERT_BLOCK, side='right'), N_EXPERTS - 1)
    h_pad = jnp.concatenate([h, jnp.zeros((1, d), h.dtype)], axis=0)

    def expert_block(args):
        tok_b, gate_b, e = args
        xb = h_pad[tok_b]
        glu = jnp.minimum(xb @ w_gate[e] + b_gate[e], SWIGLU_LIMIT)
        lin = jnp.clip(xb @ w_up[e] + b_up[e], -SWIGLU_LIMIT, SWIGLU_LIMIT)
        act = glu * jax.nn.sigmoid(SWIGLU_ALPHA * glu) * (lin + 1.0)
        return (act @ w_down[e] + b_down[e]).astype(F32) * gate_b[:, None]

    yb = lax.map(expert_block, (slot_tok.reshape(n_blocks, EXPERT_BLOCK), slot_gate.reshape(n_blocks, EXPERT_BLOCK), block_e))
    out = jnp.zeros((n_tok + 1, d), F32).at[slot_tok].add(yb.reshape(n_slots, d))
    return out[:n_tok].astype(h.dtype)


def setup_inputs(seed: int = 0) -> dict:
    key = jax.random.key(seed)
    keys = iter(jax.random.split(key, 64))
    D = D_MODEL

    def nrm(shape, scale=1.0):
        return scale * jax.random.normal(next(keys), shape, F32)

    def gain(shape):
        return 1.0 + nrm(shape, 0.1)

    return {
        'x': nrm((BATCH, SEQ, D)),
        'c': nrm((BATCH, D)),
        'ctx': nrm((BATCH, CTX_LEN, D)),
        'c_ctx': nrm((D,)),
        'w_mod': nrm((DEPTH, D, N_MOD * D), 0.5 * D ** -0.5),
        'b_mod': nrm((DEPTH, N_MOD * D), 0.02),
        'norm_g': gain((DEPTH, 4, D)),
        'ab_w_in': nrm((N_EVEN, D, AB_IN), D ** -0.5),
        'ab_q_gain': gain((N_EVEN, HEAD_DIM)),
        'ab_k_gain': gain((N_EVEN, HEAD_DIM)),
        'pool_w': nrm((N_EVEN, N_POOL_GROUPS, POOL_GROUP, POOL_GROUP), POOL_GROUP ** -0.5),
        'pool_scale': gain((N_EVEN, POOL_W)),
        'ab_w_out': nrm((N_EVEN, AB_MIX, D), AB_MIX ** -0.5),
        'rw_mix': jax.random.uniform(next(keys), (N_ODD, N_SHIFT, D), F32),
        'rw_wr': nrm((N_ODD, D, D), D ** -0.5),
        'rw_wk': nrm((N_ODD, D, D), D ** -0.5),
        'rw_wv': nrm((N_ODD, D, D), D ** -0.5),
        'rw_wo': nrm((N_ODD, D, D), D ** -0.5),
        'rw_w0': jnp.linspace(-6.0, -1.0, D, dtype=F32) + nrm((N_ODD, N_DIR, D), 0.1),
        'rw_w1': nrm((N_ODD, N_DIR, D, DECAY_LORA), D ** -0.5),
        'rw_w2': nrm((N_ODD, N_DIR, DECAY_LORA, D), 0.5 * DECAY_LORA ** -0.5),
        'rw_a0': nrm((N_ODD, N_DIR, D), 0.1),
        'rw_a1': nrm((N_ODD, N_DIR, D, AAA_LORA), D ** -0.5),
        'rw_a2': nrm((N_ODD, N_DIR, AAA_LORA, D), 0.5 * AAA_LORA ** -0.5),
        'rw_g1': nrm((N_ODD, D, GATE_LORA), D ** -0.5),
        'rw_g2': nrm((N_ODD, GATE_LORA, D), GATE_LORA ** -0.5),
        'rw_k_k': 0.85 + nrm((N_ODD, D), 0.05),
        'rw_k_a': 1.0 + nrm((N_ODD, D), 0.05),
        'rw_r_k': nrm((N_ODD, RWKV_HEADS, RWKV_HEAD), 0.1),
        'rw_ln_w': gain((N_ODD, D)),
        'rw_ln_b': nrm((N_ODD, D), 0.02),
        'router_w': nrm((DEPTH, D, N_EXPERTS), D ** -0.5),
        'router_b': nrm((DEPTH, N_EXPERTS), 0.01),
        'moe_w_gate': nrm((DEPTH, N_EXPERTS, D, D_EXPERT), D ** -0.5),
        'moe_b_gate': nrm((DEPTH, N_EXPERTS, D_EXPERT), 0.01),
        'moe_w_up': nrm((DEPTH, N_EXPERTS, D, D_EXPERT), D ** -0.5),
        'moe_b_up': nrm((DEPTH, N_EXPERTS, D_EXPERT), 0.01),
        'moe_w_down': nrm((DEPTH, N_EXPERTS, D_EXPERT, D), D_EXPERT ** -0.5),
        'moe_b_down': nrm((DEPTH, N_EXPERTS, D), 0.01),
    }


def reference(x, c, ctx, c_ctx, w_mod, b_mod, norm_g,
              ab_w_in, ab_q_gain, ab_k_gain, pool_w, pool_scale, ab_w_out,
              rw_mix, rw_wr, rw_wk, rw_wv, rw_wo, rw_w0, rw_w1, rw_w2, rw_a0, rw_a1, rw_a2,
              rw_g1, rw_g2, rw_k_k, rw_k_a, rw_r_k, rw_ln_w, rw_ln_b,
              router_w, router_b, moe_w_gate, moe_b_gate, moe_w_up, moe_b_up, moe_w_down, moe_b_down):
    bn, n, d = x.shape
    n_ctx = ctx.shape[1]
    ang_row, ang_col = grid_angles(n)
    s_lat = jax.nn.silu(c.astype(F32))
    s_ctx = jax.nn.silu(c_ctx.astype(F32))
    x_lat, x_ctx = x, ctx
    for i in range(DEPTH):
        last = i == DEPTH - 1
        j = i // 2
        m_lat = jnp.split((s_lat @ w_mod[i] + b_mod[i]).astype(x.dtype)[:, None, :], N_MOD, axis=-1)
        m_ctx = jnp.split((s_ctx @ w_mod[i] + b_mod[i]).astype(x.dtype), N_MOD, axis=-1)
        g_pre_m, g_post_m, g_pre_f, g_post_f = norm_g[i]
        h_lat = modulate(rms_norm(x_lat, g_pre_m), m_lat[0], m_lat[1])
        h_ctx = modulate(rms_norm(x_ctx, g_pre_m), m_ctx[0], m_ctx[1])
        if i % 2 == 0:
            y_lat, y_ctx = attn_pool_mixer(h_lat, h_ctx, ang_row, ang_col, ab_w_in[j], ab_q_gain[j], ab_k_gain[j],
                                           pool_w[j], pool_scale[j], ab_w_out[j], not last)
        else:
            y_lat, y_ctx = rwkv_mixer(h_lat, h_ctx, rw_mix[j], rw_wr[j], rw_wk[j], rw_wv[j], rw_wo[j],
                                      rw_w0[j], rw_w1[j], rw_w2[j], rw_a0[j], rw_a1[j], rw_a2[j],
                                      rw_g1[j], rw_g2[j], rw_k_k[j], rw_k_a[j], rw_r_k[j], rw_ln_w[j], rw_ln_b[j], not last)
        x_lat = x_lat + m_lat[2] * rms_norm(y_lat, g_post_m)
        f_lat = modulate(rms_norm(x_lat, g_pre_f), m_lat[3], m_lat[4])
        moe_p = (router_w[i], router_b[i], moe_w_gate[i], moe_b_gate[i], moe_w_up[i], moe_b_up[i],
                 moe_w_down[i], moe_b_down[i])
        if last:
            ff = rms_norm(moe_ffn(f_lat.reshape(bn * n, d), *moe_p), g_post_f)
            x_lat = x_lat + m_lat[5] * ff.reshape(bn, n, d)
        else:
            x_ctx = x_ctx + m_ctx[2] * rms_norm(y_ctx, g_post_m)
            f_ctx = modulate(rms_norm(x_ctx, g_pre_f), m_ctx[3], m_ctx[4])
            tokens = jnp.concatenate([f_lat.reshape(bn * n, d), f_ctx.reshape(bn * n_ctx, d)], axis=0)
            ff = rms_norm(moe_ffn(tokens, *moe_p), g_post_f)
            x_lat = x_lat + m_lat[5] * ff[:bn * n].reshape(bn, n, d)
            x_ctx = x_ctx + m_ctx[5] * ff[bn * n:].reshape(bn, n_ctx, d)
    return x_lat
```

```python
import functools

import jax
import jax.numpy as jnp
from jax import lax
from jax.experimental import pallas as pl
from jax.experimental.pallas import tpu as pltpu

F32 = jnp.float32
BF16 = jnp.bfloat16

TM = 256
TL = 128
TE = 256
LANES = 128
HALO = 8
VMEM_LIMIT = 56 * 1024 * 1024

RMS_EPS = 1e-6
GN_EPS = 64e-5
HEAD_DIM = 128
N_HEADS = 8
N_KV_HEADS = 2
GQA_GROUP = N_HEADS // N_KV_HEADS
GRID_W = 64
ROPE_THETA = 10000.0
POOL_WINDOWS = (2, 4, 8, 16)
RWKV_HEAD = 64
LORA_PAD = 128
N_EXPERTS = 32
TOP_K = 4
SWIGLU_LIMIT = 7.0
SWIGLU_ALPHA = 1.702


def _cparams(*sem):
    return pltpu.CompilerParams(dimension_semantics=sem, vmem_limit_bytes=VMEM_LIMIT)


def _rms(x, g):
    return x * lax.rsqrt(jnp.mean(x * x, axis=-1, keepdims=True) + RMS_EPS) * g


def _sigmoid(x):
    return 1.0 / (1.0 + jnp.exp(-x))


def _dot(a, b):
    return jnp.dot(a, b, preferred_element_type=F32)


def _segsum(x, j_ref):
    outs = []
    for c in range(x.shape[1] // LANES):
        xc = x[:, c * LANES:(c + 1) * LANES]
        hi = xc.astype(BF16)
        lo = (xc - hi.astype(F32)).astype(BF16)
        outs.append(_dot(hi, j_ref[...]) + _dot(lo, j_ref[...]))
    return jnp.concatenate(outs, axis=1)


def _mod_kernel(c_ref, w_ref, b_ref, o_ref):
    c = c_ref[...]
    s = c * _sigmoid(c)
    o_ref[0] = _dot(s.astype(BF16), w_ref[0].astype(BF16)) + b_ref[0]


def _mod_rows(cvec, w_mod, b_mod):
    depth, d, n = w_mod.shape
    tn = 1024
    rows = cvec.shape[0]
    return pl.pallas_call(
        _mod_kernel,
        grid=(depth, n // tn),
        in_specs=[
            pl.BlockSpec((rows, d), lambda i, j: (0, 0)),
            pl.BlockSpec((1, d, tn), lambda i, j: (i, 0, j)),
            pl.BlockSpec((1, 1, tn), lambda i, j: (i, 0, j)),
        ],
        out_specs=pl.BlockSpec((1, rows, tn), lambda i, j: (i, 0, j)),
        out_shape=jax.ShapeDtypeStruct((depth, rows, n), F32),
        compiler_params=_cparams("arbitrary", "arbitrary"),
        name="mod_rows",
    )(cvec, w_mod, b_mod.reshape(depth, 1, n))


def _inproj_kernel(x_ref, mod_ref, g_ref, w_ref, qg_ref, kg_ref, cos_ref, sin_ref,
                   q_ref, k_ref, v_ref, pin_ref):
    x = x_ref[0]
    h = _rms(x, g_ref[0:1, :]) * (1.0 + mod_ref[0, 1:2, :]) + mod_ref[0, 0:1, :]
    u = _dot(h.astype(BF16), w_ref[...])
    cos = cos_ref[...]
    sin = sin_ref[...]
    lane = lax.broadcasted_iota(jnp.int32, (TM, HEAD_DIM), 1)
    first_half = (lane & 32) == 0

    def head(xh, gain, scale):
        y = _rms(xh, gain)
        sw = jnp.where(first_half, pltpu.roll(y, 96, 1), pltpu.roll(y, 32, 1))
        return ((y * cos + sw * sin) * scale).astype(BF16)

    attn_w = N_HEADS * HEAD_DIM
    kv_w = N_KV_HEADS * HEAD_DIM
    for hh in range(N_HEADS):
        sl = slice(hh * HEAD_DIM, (hh + 1) * HEAD_DIM)
        q_ref[0, :, sl] = head(u[:, sl], qg_ref[...], HEAD_DIM ** -0.5)
    for hh in range(N_KV_HEADS):
        sl = slice(hh * HEAD_DIM, (hh + 1) * HEAD_DIM)
        k_ref[0, :, sl] = head(u[:, attn_w + hh * HEAD_DIM:attn_w + (hh + 1) * HEAD_DIM], kg_ref[...], 1.0)
    v_ref[0] = u[:, attn_w + kv_w:attn_w + 2 * kv_w].astype(BF16)
    pin_ref[0] = u[:, attn_w + 2 * kv_w:]


def _inproj(x_all, mods, g4, w_in_bf, q_gain, k_gain, cos_t, sin_t, n_lat_tiles):
    bn, tt, d = x_all.shape
    nt = tt // TM
    attn_w = N_HEADS * HEAD_DIM
    kv_w = N_KV_HEADS * HEAD_DIM
    pool_w = w_in_bf.shape[1] - attn_w - 2 * kv_w
    mod_map = lambda b, t: (jnp.where(t < n_lat_tiles, b, bn), 0, 0)
    return pl.pallas_call(
        _inproj_kernel,
        grid=(bn, nt),
        in_specs=[
            pl.BlockSpec((1, TM, d), lambda b, t: (b, t, 0)),
            pl.BlockSpec((1, 6, d), mod_map),
            pl.BlockSpec((4, d), lambda b, t: (0, 0)),
            pl.BlockSpec(w_in_bf.shape, lambda b, t: (0, 0)),
            pl.BlockSpec((1, HEAD_DIM), lambda b, t: (0, 0)),
            pl.BlockSpec((1, HEAD_DIM), lambda b, t: (0, 0)),
            pl.BlockSpec((TM, HEAD_DIM), lambda b, t: (t, 0)),
            pl.BlockSpec((TM, HEAD_DIM), lambda b, t: (t, 0)),
        ],
        out_specs=[
            pl.BlockSpec((1, TM, attn_w), lambda b, t: (b, t, 0)),
            pl.BlockSpec((1, TM, kv_w), lambda b, t: (b, t, 0)),
            pl.BlockSpec((1, TM, kv_w), lambda b, t: (b, t, 0)),
            pl.BlockSpec((1, TM, pool_w), lambda b, t: (b, t, 0)),
        ],
        out_shape=[
            jax.ShapeDtypeStruct((bn, tt, attn_w), BF16),
            jax.ShapeDtypeStruct((bn, tt, kv_w), BF16),
            jax.ShapeDtypeStruct((bn, tt, kv_w), BF16),
            jax.ShapeDtypeStruct((bn, tt, pool_w), F32),
        ],
        compiler_params=_cparams("arbitrary", "arbitrary"),
        name="inproj",
    )(x_all, mods, g4, w_in_bf, q_gain.reshape(1, -1), k_gain.reshape(1, -1), cos_t, sin_t)


def _attn_kernel(q_ref, k_ref, v_ref, o_ref, m_sc, l_sc, acc_sc, *, n_lat_tiles, n_tiles):
    t = pl.program_id(2)
    q = q_ref[0]
    qs = jnp.concatenate([q[:, i * HEAD_DIM:(i + 1) * HEAD_DIM] for i in range(GQA_GROUP)], axis=0)
    m_sc[...] = jnp.full(m_sc.shape, -jnp.inf, F32)
    l_sc[...] = jnp.zeros(l_sc.shape, F32)
    acc_sc[...] = jnp.zeros(acc_sc.shape, F32)
    first = jnp.where(t < n_lat_tiles, 0, n_lat_tiles)

    def body(c, carry):
        start = pl.multiple_of(c * TM, TM)
        kc = k_ref[0, pl.ds(start, TM), :]
        vc = v_ref[0, pl.ds(start, TM), :]
        s = lax.dot_general(qs, kc, (((1,), (1,)), ((), ())), preferred_element_type=F32)
        m_prev = m_sc[...]
        m_new = jnp.maximum(m_prev, jnp.max(s, axis=-1, keepdims=True))
        alpha = jnp.exp(m_prev - m_new)
        p = jnp.exp(s - m_new)
        l_sc[...] = alpha * l_sc[...] + jnp.sum(p, axis=-1, keepdims=True)
        acc_sc[...] = alpha * acc_sc[...] + _dot(p.astype(BF16), vc)
        m_sc[...] = m_new
        return carry

    lax.fori_loop(first, n_tiles, body, 0)
    o = acc_sc[...] / l_sc[...]
    for i in range(GQA_GROUP):
        o_ref[0, :, i * HEAD_DIM:(i + 1) * HEAD_DIM] = o[i * TM:(i + 1) * TM].astype(BF16)


def _attention(q, k, v, n_lat_tiles):
    bn, tt, attn_w = q.shape
    nt = tt // TM
    gw = GQA_GROUP * HEAD_DIM
    kern = functools.partial(_attn_kernel, n_lat_tiles=n_lat_tiles, n_tiles=nt)
    return pl.pallas_call(
        kern,
        grid=(bn, N_KV_HEADS, nt),
        in_specs=[
            pl.BlockSpec((1, TM, gw), lambda b, g, t: (b, t, g)),
            pl.BlockSpec((1, tt, HEAD_DIM), lambda b, g, t: (b, 0, g)),
            pl.BlockSpec((1, tt, HEAD_DIM), lambda b, g, t: (b, 0, g)),
        ],
        out_specs=pl.BlockSpec((1, TM, gw), lambda b, g, t: (b, t, g)),
        out_shape=jax.ShapeDtypeStruct((bn, tt, attn_w), BF16),
        scratch_shapes=[
            pltpu.VMEM((GQA_GROUP * TM, 1), F32),
            pltpu.VMEM((GQA_GROUP * TM, 1), F32),
            pltpu.VMEM((GQA_GROUP * TM, HEAD_DIM), F32),
        ],
        compiler_params=_cparams("arbitrary", "arbitrary", "arbitrary"),
        name="attention",
    )(q, k, v)


def _post_mixer(y, x, mod_ref, g_ref):
    xn = x + mod_ref[0, 2:3, :] * _rms(y, g_ref[1:2, :])
    f = _rms(xn, g_ref[2:3, :]) * (1.0 + mod_ref[0, 4:5, :]) + mod_ref[0, 3:4, :]
    return xn, f


def _mix0_kernel(o_ref, pin_ref, pprev_ref, pnext_ref, pw_ref, ps_ref, wout_ref, x_ref, mod_ref, g_ref,
                 xo_ref, f_ref, ext_sc, cat_sc, *, n_lat_tiles, lat_len, ctx_len):
    t = pl.program_id(1)
    is_ctx = t >= n_lat_tiles
    seq_first = jnp.logical_or(t == 0, is_ctx)
    seq_last = jnp.logical_or(t == n_lat_tiles - 1, is_ctx)
    pin = pin_ref[0]
    ext_sc[0:HALO, :] = jnp.where(seq_first, 0.0, pprev_ref[0])
    ext_sc[HALO:HALO + TM, :] = pin
    ext_sc[HALO + TM:, :] = jnp.where(seq_last, 0.0, pnext_ref[0])
    row = lax.broadcasted_iota(jnp.int32, (TM, 1), 0)
    pos = jnp.where(is_ctx, 0, t * TM) + row
    n = jnp.where(is_ctx, ctx_len, lat_len)
    gw = pin.shape[1] // len(POOL_WINDOWS)
    attn_w = o_ref.shape[2]
    cat_sc[:, 0:attn_w] = o_ref[0]
    for gi, w in enumerate(POOL_WINDOWS):
        sl = slice(gi * gw, (gi + 1) * gw)
        base = HALO - w // 2
        acc = ext_sc[base:base + TM, sl]
        for j in range(1, w):
            acc = acc + ext_sc[base + j:base + j + TM, sl]
        cnt = (jnp.minimum(pos + w // 2, n) - jnp.maximum(pos - w // 2, 0)).astype(F32)
        m = acc / cnt - pin[:, sl]
        y = _dot(m.astype(BF16), pw_ref[gi]) * ps_ref[:, sl]
        cat_sc[:, attn_w + gi * gw:attn_w + (gi + 1) * gw] = y.astype(BF16)
    y = _dot(cat_sc[...], wout_ref[...])
    xn, f = _post_mixer(y, x_ref[0], mod_ref, g_ref)
    xo_ref[0] = xn
    f_ref[0] = f


def _mix0_out(o, pin, pool_w_bf, pool_scale, w_out_bf, x_all, mods, g4, n_lat_tiles, lat_len, ctx_len):
    bn, tt, d = x_all.shape
    nt = tt // TM
    attn_w = o.shape[2]
    pw = pin.shape[2]
    hb = TM // HALO
    n_hblk = tt // HALO
    mod_map = lambda b, t: (jnp.where(t < n_lat_tiles, b, bn), 0, 0)
    kern = functools.partial(_mix0_kernel, n_lat_tiles=n_lat_tiles, lat_len=lat_len, ctx_len=ctx_len)
    return pl.pallas_call(
        kern,
        grid=(bn, nt),
        in_specs=[
            pl.BlockSpec((1, TM, attn_w), lambda b, t: (b, t, 0)),
            pl.BlockSpec((1, TM, pw), lambda b, t: (b, t, 0)),
            pl.BlockSpec((1, HALO, pw), lambda b, t: (b, jnp.maximum(t * hb - 1, 0), 0)),
            pl.BlockSpec((1, HALO, pw), lambda b, t: (b, jnp.minimum((t + 1) * hb, n_hblk - 1), 0)),
            pl.BlockSpec(pool_w_bf.shape, lambda b, t: (0, 0, 0)),
            pl.BlockSpec((1, pw), lambda b, t: (0, 0)),
            pl.BlockSpec(w_out_bf.shape, lambda b, t: (0, 0)),
            pl.BlockSpec((1, TM, d), lambda b, t: (b, t, 0)),
            pl.BlockSpec((1, 6, d), mod_map),
            pl.BlockSpec((4, d), lambda b, t: (0, 0)),
        ],
        out_specs=[
            pl.BlockSpec((1, TM, d), lambda b, t: (b, t, 0)),
            pl.BlockSpec((1, TM, d), lambda b, t: (b, t, 0)),
        ],
        out_shape=[jax.ShapeDtypeStruct((bn, tt, d), F32), jax.ShapeDtypeStruct((bn, tt, d), F32)],
        scratch_shapes=[pltpu.VMEM((TM + 2 * HALO, pw), F32), pltpu.VMEM((TM, attn_w + pw), BF16)],
        compiler_params=_cparams("arbitrary", "arbitrary"),
        name="pool_outproj",
    )(o, pin, pin, pin, pool_w_bf, pool_scale.reshape(1, -1), w_out_bf, x_all, mods, g4)


def _router_kernel(f_ref, rw_ref, rb_ref, idx_ref, gate_ref, rank_ref, cnt_ref, carry_sc):
    i = pl.program_id(0)

    @pl.when(i == 0)
    def _():
        carry_sc[...] = jnp.zeros(carry_sc.shape, F32)

    lane = lax.broadcasted_iota(jnp.int32, (TM, LANES), 1).astype(F32)
    logits = _dot(f_ref[...].astype(BF16), rw_ref[...]) + rb_ref[...]
    l = jnp.where(lane < N_EXPERTS, logits, -jnp.inf)
    vals, sels, idxs = [], [], []
    for _ in range(TOP_K):
        m = jnp.max(l, axis=-1, keepdims=True)
        idx = jnp.min(jnp.where(l == m, lane, float(LANES)), axis=-1, keepdims=True)
        sel = lane == idx
        vals.append(m)
        idxs.append(idx)
        sels.append(sel)
        l = jnp.where(sel, -jnp.inf, l)
    es = [jnp.exp(v - vals[0]) for v in vals]
    denom = es[0] + es[1] + es[2] + es[3]
    onehot = jnp.zeros((TM, LANES), F32)
    for sel in sels:
        onehot = onehot + jnp.where(sel, 1.0, 0.0)
    r_i = lax.broadcasted_iota(jnp.int32, (TM, TM), 0)
    c_i = lax.broadcasted_iota(jnp.int32, (TM, TM), 1)
    tril = jnp.where(c_i < r_i, 1.0, 0.0).astype(BF16)
    before = _dot(tril, onehot.astype(BF16)) + carry_sc[0:1, :]
    idx_out = jnp.zeros((TM, LANES), F32)
    gate_out = jnp.zeros((TM, LANES), F32)
    rank_out = jnp.zeros((TM, LANES), F32)
    for k in range(TOP_K):
        rk = jnp.sum(jnp.where(sels[k], before, 0.0), axis=-1, keepdims=True)
        idx_out = jnp.where(lane == k, idxs[k], idx_out)
        gate_out = jnp.where(lane == k, es[k] / denom, gate_out)
        rank_out = jnp.where(lane == k, rk, rank_out)
    idx_ref[...] = idx_out.astype(jnp.int32)
    gate_ref[...] = gate_out
    rank_ref[...] = rank_out.astype(jnp.int32)
    carry_sc[...] = carry_sc[...] + jnp.sum(onehot, axis=0, keepdims=True)
    cnt_ref[...] = carry_sc[...].astype(jnp.int32)


def _router(f2, router_w, router_b):
    n, d = f2.shape
    rw = jnp.zeros((d, LANES), BF16).at[:, :N_EXPERTS].set(router_w.astype(BF16))
    rb = jnp.zeros((1, LANES), F32).at[0, :N_EXPERTS].set(router_b)
    tile = pl.BlockSpec((TM, LANES), lambda i: (i, 0))
    return pl.pallas_call(
        _router_kernel,
        grid=(n // TM,),
        in_specs=[
            pl.BlockSpec((TM, d), lambda i: (i, 0)),
            pl.BlockSpec((d, LANES), lambda i: (0, 0)),
            pl.BlockSpec((1, LANES), lambda i: (0, 0)),
        ],
        out_specs=[tile, tile, tile, pl.BlockSpec((HALO, LANES), lambda i: (0, 0))],
        out_shape=[
            jax.ShapeDtypeStruct((n, LANES), jnp.int32),
            jax.ShapeDtypeStruct((n, LANES), F32),
            jax.ShapeDtypeStruct((n, LANES), jnp.int32),
            jax.ShapeDtypeStruct((HALO, LANES), jnp.int32),
        ],
        scratch_shapes=[pltpu.VMEM((HALO, LANES), F32)],
        compiler_params=_cparams("arbitrary"),
        name="router",
    )(f2, rw, rb)


def _expert_kernel(be_ref, tok_ref, nb_ref, f_hbm, wg_ref, bg_ref, wu_ref, bu_ref, wd_ref, bd_ref,
                   y_ref, xbuf, sem):
    s = pl.program_id(0)
    nb = nb_ref[0]

    def issue(blk, slot):
        base = blk * TE

        def body(r, carry):
            tok = tok_ref[base + r]
            pltpu.make_async_copy(f_hbm.at[pl.ds(tok, 1)], xbuf.at[slot, pl.ds(r, 1)], sem.at[slot]).start()
            return carry

        lax.fori_loop(0, TE, body, 0)

    @pl.when(s == 0)
    def _():
        issue(0, 0)

    @pl.when(s + 1 < nb)
    def _():
        issue(s + 1, (s + 1) % 2)

    @pl.when(s < nb)
    def _():
        slot = s % 2
        pltpu.make_async_copy(f_hbm.at[pl.ds(0, TE)], xbuf.at[slot], sem.at[slot]).wait()
        x = xbuf[slot].astype(BF16)
        glu = jnp.minimum(_dot(x, wg_ref[0]) + bg_ref[0], SWIGLU_LIMIT)
        lin = jnp.clip(_dot(x, wu_ref[0]) + bu_ref[0], -SWIGLU_LIMIT, SWIGLU_LIMIT)
        act = glu * _sigmoid(SWIGLU_ALPHA * glu) * (lin + 1.0)
        y_ref[...] = _dot(act.astype(BF16), wd_ref[0]) + bd_ref[0]

    @pl.when(s >= nb)
    def _():
        y_ref[...] = jnp.zeros(y_ref.shape, F32)


def _experts(f2, block_e, slot_tok, n_used, wg, bg, wu, bu, wd, bd, n_blocks):
    n, d = f2.shape
    ne, _, de = wg.shape
    grid_spec = pltpu.PrefetchScalarGridSpec(
        num_scalar_prefetch=3,
        grid=(n_blocks,),
        in_specs=[
            pl.BlockSpec(memory_space=pl.ANY),
            pl.BlockSpec((1, d, de), lambda s, be, tok, nb: (be[s], 0, 0)),
            pl.BlockSpec((1, 1, de), lambda s, be, tok, nb: (be[s], 0, 0)),
            pl.BlockSpec((1, d, de), lambda s, be, tok, nb: (be[s], 0, 0)),
            pl.BlockSpec((1, 1, de), lambda s, be, tok, nb: (be[s], 0, 0)),
            pl.BlockSpec((1, de, d), lambda s, be, tok, nb: (be[s], 0, 0)),
            pl.BlockSpec((1, 1, d), lambda s, be, tok, nb: (be[s], 0, 0)),
        ],
        out_specs=pl.BlockSpec((TE, d), lambda s, be, tok, nb: (s, 0)),
        scratch_shapes=[pltpu.VMEM((2, TE, d), F32), pltpu.SemaphoreType.DMA((2,))],
    )
    return pl.pallas_call(
        _expert_kernel,
        grid_spec=grid_spec,
        out_shape=jax.ShapeDtypeStruct((n_blocks * TE, d), F32),
        compiler_params=_cparams("arbitrary"),
        name="experts",
    )(block_e, slot_tok, n_used, f2, wg, bg.reshape(ne, 1, de), wu, bu.reshape(ne, 1, de), wd, bd.reshape(ne, 1, d))


def _combine_kernel(dest_ref, tmod_ref, yb_hbm, gate_ref, x_ref, mod_ref, g_ref, xo_ref, buf, sem):
    i = pl.program_id(0)
    nt = pl.num_programs(0)

    def issue(tile, slot):
        base = tile * TM * TOP_K

        def body(r, carry):
            for k in range(TOP_K):
                dst = dest_ref[base + r * TOP_K + k]
                pltpu.make_async_copy(yb_hbm.at[pl.ds(dst, 1)], buf.at[slot, k, pl.ds(r, 1)], sem.at[slot]).start()
            return carry

        lax.fori_loop(0, TM, body, 0)

    @pl.when(i == 0)
    def _():
        issue(0, 0)

    @pl.when(i + 1 < nt)
    def _():
        issue(i + 1, (i + 1) % 2)

    slot = i % 2
    for k in range(TOP_K):
        pltpu.make_async_copy(yb_hbm.at[pl.ds(0, TM)], buf.at[slot, k], sem.at[slot]).wait()
    gate = gate_ref[...]
    out = gate[:, 0:1] * buf[slot, 0]
    for k in range(1, TOP_K):
        out = out + gate[:, k:k + 1] * buf[slot, k]
    xo_ref[...] = x_ref[...] + mod_ref[0, 5:6, :] * _rms(out, g_ref[3:4, :])


def _combine(yb, dest, tile_mod, gates, x2, mods, g4):
    n, d = x2.shape
    grid_spec = pltpu.PrefetchScalarGridSpec(
        num_scalar_prefetch=2,
        grid=(n // TM,),
        in_specs=[
            pl.BlockSpec(memory_space=pl.ANY),
            pl.BlockSpec((TM, LANES), lambda i, dst, tm: (i, 0)),
            pl.BlockSpec((TM, d), lambda i, dst, tm: (i, 0)),
            pl.BlockSpec((1, 6, d), lambda i, dst, tm: (tm[i], 0, 0)),
            pl.BlockSpec((4, d), lambda i, dst, tm: (0, 0)),
        ],
        out_specs=pl.BlockSpec((TM, d), lambda i, dst, tm: (i, 0)),
        scratch_shapes=[pltpu.VMEM((2, TOP_K, TM, d), F32), pltpu.SemaphoreType.DMA((2,))],
    )
    return pl.pallas_call(
        _combine_kernel,
        grid_spec=grid_spec,
        out_shape=jax.ShapeDtypeStruct((n, d), F32),
        compiler_params=_cparams("arbitrary"),
        name="moe_combine",
    )(dest, tile_mod, yb, gates, x2, mods, g4)


def _moe(f2, x2, tile_mod, mods, g4, router_w, router_b, wg, bg, wu, bu, wd, bd):
    n, d = f2.shape
    idx, gates, rank, cnt = _router(f2, router_w, router_b)
    idx = idx[:, :TOP_K]
    rank = rank[:, :TOP_K]
    counts = cnt[0, :N_EXPERTS]
    padded = (counts + TE - 1) // TE * TE
    pend = jnp.cumsum(padded)
    pstart = pend - padded
    dest = (pstart[idx] + rank).reshape(-1).astype(jnp.int32)
    n_asg = n * TOP_K
    n_blocks = -(-(n_asg + N_EXPERTS * (TE - 1)) // TE)
    n_used = (pend[-1] // TE).astype(jnp.int32)
    blk = jnp.arange(n_blocks, dtype=jnp.int32)
    block_e = jnp.searchsorted(pend, jnp.minimum(blk, n_used - 1) * TE, side="right").astype(jnp.int32)
    block_e = jnp.minimum(block_e, N_EXPERTS - 1)
    slot_tok = jnp.zeros((n_blocks * TE,), jnp.int32).at[dest].set(jnp.repeat(jnp.arange(n, dtype=jnp.int32), TOP_K))
    yb = _experts(f2, block_e, slot_tok, n_used.reshape(1), wg.astype(BF16), bg, wu.astype(BF16), bu,
                  wd.astype(BF16), bd, n_blocks)
    return _combine(yb, dest, tile_mod, gates, x2, mods, g4)


def _normmod_kernel(x_ref, mod_ref, g_ref, h_ref):
    h_ref[0] = _rms(x_ref[0], g_ref[0:1, :]) * (1.0 + mod_ref[0, 1:2, :]) + mod_ref[0, 0:1, :]


def _normmod(x_all, mods, g4, n_lat_tiles):
    bn, tt, d = x_all.shape
    mod_map = lambda b, t: (jnp.where(t < n_lat_tiles, b, bn), 0, 0)
    tile = pl.BlockSpec((1, TM, d), lambda b, t: (b, t, 0))
    return pl.pallas_call(
        _normmod_kernel,
        grid=(bn, tt // TM),
        in_specs=[tile, pl.BlockSpec((1, 6, d), mod_map), pl.BlockSpec((4, d), lambda b, t: (0, 0))],
        out_specs=tile,
        out_shape=jax.ShapeDtypeStruct((bn, tt, d), F32),
        compiler_params=_cparams("arbitrary", "arbitrary"),
        name="norm_mod",
    )(x_all, mods, g4)


def _centred_shift(h, hprev_ref, hnext_ref, t, n_lat_tiles, n_tiles):
    seq_first = jnp.logical_or(t == 0, t == n_lat_tiles)
    seq_last = jnp.logical_or(t == n_lat_tiles - 1, t == n_tiles - 1)
    prev_row = jnp.where(seq_first, 0.0, hprev_ref[0, HALO - 1:HALO, :])
    next_row = jnp.where(seq_last, 0.0, hnext_ref[0, 0:1, :])
    tm = h.shape[0]
    row = lax.broadcasted_iota(jnp.int32, (tm, 1), 0)
    prev = jnp.where(row == 0, prev_row, pltpu.roll(h, 1, 0))
    nxt = jnp.where(row == tm - 1, next_row, pltpu.roll(h, tm - 1, 0))
    return 0.5 * (prev + nxt) - h


def _halo_specs(d, tt, tm):
    hb = tm // HALO
    n_hblk = tt // HALO
    return [
        pl.BlockSpec((1, tm, d), lambda *a: (a[-2], a[-1], 0)),
        pl.BlockSpec((1, HALO, d), lambda *a: (a[-2], jnp.maximum(a[-1] * hb - 1, 0), 0)),
        pl.BlockSpec((1, HALO, d), lambda *a: (a[-2], jnp.minimum((a[-1] + 1) * hb, n_hblk - 1), 0)),
    ]


def _shiftproj_kernel(h_ref, hprev_ref, hnext_ref, mix_ref, w_ref, o_ref, *, n_lat_tiles):
    t = pl.program_id(2)
    h = h_ref[0]
    xx = _centred_shift(h, hprev_ref, hnext_ref, t, n_lat_tiles, pl.num_programs(2))
    o_ref[0, 0] =_dot((h + xx * mix_ref[0]).astype(BF16), w_ref[0])


def _shiftproj(h, mix3, w3_bf, n_lat_tiles):
    bn, tt, d = h.shape
    nj = w3_bf.shape[0]
    kern = functools.partial(_shiftproj_kernel, n_lat_tiles=n_lat_tiles)
    return pl.pallas_call(
        kern,
        grid=(nj, bn, tt // TM),
        in_specs=_halo_specs(d, tt, TM) + [
            pl.BlockSpec((1, 1, d), lambda j, b, t: (j, 0, 0)),
            pl.BlockSpec((1, d, d), lambda j, b, t: (j, 0, 0)),
        ],
        out_specs=pl.BlockSpec((1, 1, TM, d), lambda j, b, t: (j, b, t, 0)),
        out_shape=jax.ShapeDtypeStruct((nj, bn, tt, d), F32),
        compiler_params=_cparams("arbitrary", "arbitrary", "arbitrary"),
        name="shift_proj",
    )(h, h, h, mix3, w3_bf)


def _softplus(x):
    return jnp.maximum(x, 0.0) + jnp.log(1.0 + jnp.exp(-jnp.abs(x)))


def _lora_kernel(h_ref, hprev_ref, hnext_ref, k_ref, mix_ref, w1_ref, a1_ref, g1_ref, w2_ref, a2_ref, g2_ref,
                 w0_ref, a0_ref, kk_ref, ka_ref, j_ref,
                 dec_ref, kd_ref, bb_ref, nkk_ref, gate_ref, *, n_lat_tiles):
    t = pl.program_id(1)
    h = h_ref[0]
    xx = _centred_shift(h, hprev_ref, hnext_ref, t, n_lat_tiles, pl.num_programs(1))
    hw =jnp.tanh(_dot((h + xx * mix_ref[0:1, :]).astype(BF16), w1_ref[...]))
    ha = _dot((h + xx * mix_ref[1:2, :]).astype(BF16), a1_ref[...])
    hg = _sigmoid(_dot((h + xx * mix_ref[2:3, :]).astype(BF16), g1_ref[...]))
    gate_ref[0] = _dot(hg.astype(BF16), g2_ref[...])
    k = k_ref[0, 0]
    kk = k * kk_ref[...]
    ss = _segsum(kk * kk, j_ref)
    kk = kk * lax.rsqrt(jnp.maximum(ss, 1e-24))
    nkk_ref[0] = -kk
    for dd in range(2):
        sl = slice(dd * LORA_PAD, (dd + 1) * LORA_PAD)
        z = w0_ref[dd] + _dot(hw[:, sl].astype(BF16), w2_ref[dd])
        w_log = -_softplus(-z) - 0.5
        dec_ref[dd, 0] = jnp.exp(-jnp.exp(w_log))
        a = _sigmoid(a0_ref[dd] + _dot(ha[:, sl].astype(BF16), a2_ref[dd]))
        kd_ref[dd, 0] = k * (1.0 + (a - 1.0) * ka_ref[...])
        bb_ref[dd, 0] = kk * a


def _pad_lora(w_in, w_out):
    nd, d, r = w_in.shape
    wi = jnp.zeros((d, nd * LORA_PAD), BF16)
    wo = jnp.zeros((nd, LORA_PAD, d), BF16)
    for dd in range(nd):
        wi = wi.at[:, dd * LORA_PAD:dd * LORA_PAD + r].set(w_in[dd].astype(BF16))
        wo = wo.at[dd, :r].set(w_out[dd].astype(BF16))
    return wi, wo


def _seg_ones():
    i = jnp.arange(LANES) // RWKV_HEAD
    return (i[:, None] == i[None, :]).astype(BF16)


def _lora(h, rkv, mix3, w1, w2, a1, a2, g1, g2, w0, a0, k_k, k_a, n_lat_tiles):
    bn, tt, d = h.shape
    w1p, w2p = _pad_lora(w1, w2)
    a1p, a2p = _pad_lora(a1, a2)
    const2 = lambda s: pl.BlockSpec(s, lambda b, t: (0, 0))
    const3 = lambda s: pl.BlockSpec(s, lambda b, t: (0, 0, 0))
    kern = functools.partial(_lora_kernel, n_lat_tiles=n_lat_tiles * (TM // TL))
    dir_out = pl.BlockSpec((2, 1, TL, d), lambda b, t: (0, b, t, 0))
    tok_out = pl.BlockSpec((1, TL, d), lambda b, t: (b, t, 0))
    dir_shape = jax.ShapeDtypeStruct((2, bn, tt, d), F32)
    tok_shape = jax.ShapeDtypeStruct((bn, tt, d), F32)
    return pl.pallas_call(
        kern,
        grid=(bn, tt // TL),
        in_specs=_halo_specs(d, tt, TL) + [
            pl.BlockSpec((1, 1, TL, d), lambda b, t: (1, b, t, 0)),
            const2((3, d)), const2(w1p.shape), const2(a1p.shape), const2((d, g1.shape[1])),
            const3(w2p.shape), const3(a2p.shape), const2(g2.shape),
            const3((2, 1, d)), const3((2, 1, d)), const2((1, d)), const2((1, d)), const2((LANES, LANES)),
        ],
        out_specs=[dir_out, dir_out, dir_out, tok_out, tok_out],
        out_shape=[dir_shape, dir_shape, dir_shape, tok_shape, tok_shape],
        compiler_params=_cparams("arbitrary", "arbitrary"),
        name="rwkv_lora",
    )(h, h, h, rkv, mix3, w1p, a1p, g1.astype(BF16), w2p, a2p, g2.astype(BF16),
      w0.reshape(2, 1, d), a0.reshape(2, 1, d), k_k.reshape(1, d), k_a.reshape(1, d), _seg_ones())


def _scan_kernel(w_ref, k_ref, b_ref, a_ref, v_ref, r_ref, y_ref, s_sc, *, steps):
    @pl.when(pl.program_id(0) == 0)
    def _():
        s_sc[...] = jnp.zeros(s_sc.shape, F32)

    def step(i, carry):
        a = a_ref[i]
        w = w_ref[i]
        b = b_ref[i]
        k = k_ref[i]
        r = r_ref[i]
        wr = w * r
        br = jnp.sum(b * r, axis=0, keepdims=True)
        kr = jnp.sum(k * r, axis=0, keepdims=True)
        for vi in range(RWKV_HEAD):
            s = s_sc[vi]
            sa = jnp.sum(s * a, axis=0, keepdims=True)
            y0 = jnp.sum(s * wr, axis=0, keepdims=True)
            vv = v_ref[i, vi:vi + 1, :]
            s_sc[vi] = s * w + sa * b + vv * k
            y_ref[i, vi:vi + 1, :] = y0 + sa * br + vv * kr
        return carry

    lax.fori_loop(0, steps, step, 0)


def _scan(w, k, b, a, v, r, steps=32):
    tt = w.shape[0]
    blk = pl.BlockSpec((steps, RWKV_HEAD, LANES), lambda i: (i, 0, 0))
    return pl.pallas_call(
        functools.partial(_scan_kernel, steps=steps),
        grid=(tt // steps,),
        in_specs=[blk] * 6,
        out_specs=blk,
        out_shape=jax.ShapeDtypeStruct((tt, RWKV_HEAD, LANES), F32),
        scratch_shapes=[pltpu.VMEM((RWKV_HEAD, RWKV_HEAD, LANES), F32)],
        compiler_params=_cparams("arbitrary"),
        name="rwkv_scan",
    )(w, k, b, a, v, r)


def _to_chains(fwd, rev, lat_len):
    def seq(x, flip):
        lat, ctx = x[:, :lat_len], x[:, lat_len:]
        if flip:
            lat, ctx = lat[:, ::-1], ctx[:, ::-1]
        x = jnp.concatenate([ctx, lat], axis=1)
        bn, tt, d = x.shape
        x = x.reshape(bn, tt, d // RWKV_HEAD, RWKV_HEAD).transpose(1, 3, 0, 2)
        return x.reshape(tt, RWKV_HEAD, bn * (d // RWKV_HEAD))
    return jnp.concatenate([seq(fwd, False), seq(rev, True)], axis=-1)


def _from_chains(y, bn, lat_len, ctx_len):
    tt = y.shape[0]
    nh = y.shape[2] // (2 * bn)
    outs = []
    for dd in range(2):
        yd = y[ctx_len:, :, dd * bn * nh:(dd + 1) * bn * nh].reshape(lat_len, RWKV_HEAD, bn, nh)
        yd = yd.transpose(2, 0, 3, 1).reshape(bn, lat_len, nh * RWKV_HEAD)
        outs.append(yd[:, ::-1] if dd == 1 else yd)
    return outs


def _readout_kernel(y0_ref, y1_ref, r_ref, kd_ref, v_ref, gate_ref, lnw_ref, lnb_ref, rk_ref, wo_ref, j_ref,
                    x_ref, mod_ref, g_ref, xo_ref, f_ref):
    y = y0_ref[0] + y1_ref[0]
    inv_n = 1.0 / RWKV_HEAD
    mu = _segsum(y, j_ref) * inv_n
    yc = y - mu
    var = _segsum(yc * yc, j_ref) * inv_n
    yn = yc * lax.rsqrt(var + GN_EPS) * lnw_ref[...] + lnb_ref[...]
    rk = r_ref[0, 0] * rk_ref[...]
    bonus = (_segsum(rk * kd_ref[0, 0], j_ref) + _segsum(rk * kd_ref[1, 0], j_ref)) * v_ref[0, 0]
    o = (yn + bonus) * gate_ref[0]
    yo = _dot(o.astype(BF16), wo_ref[...])
    xn, f = _post_mixer(yo, x_ref[0], mod_ref, g_ref)
    xo_ref[0] = xn
    f_ref[0] = f


def _readout(y0, y1, rkv, kd, gate, ln_w, ln_b, r_k, wo_bf, x_all, mods, g4):
    bn, ll, d = y0.shape
    tok = pl.BlockSpec((1, TL, d), lambda b, t: (b, t, 0))
    row = pl.BlockSpec((1, d), lambda b, t: (0, 0))
    return pl.pallas_call(
        _readout_kernel,
        grid=(bn, ll // TL),
        in_specs=[
            tok, tok,
            pl.BlockSpec((1, 1, TL, d), lambda b, t: (0, b, t, 0)),
            pl.BlockSpec((2, 1, TL, d), lambda b, t: (0, b, t, 0)),
            pl.BlockSpec((1, 1, TL, d), lambda b, t: (2, b, t, 0)),
            tok, row, row, row,
            pl.BlockSpec((d, d), lambda b, t: (0, 0)),
            pl.BlockSpec((LANES, LANES), lambda b, t: (0, 0)),
            tok,
            pl.BlockSpec((1, 6, d), lambda b, t: (b, 0, 0)),
            pl.BlockSpec((4, d), lambda b, t: (0, 0)),
        ],
        out_specs=[tok, tok],
        out_shape=[jax.ShapeDtypeStruct((bn, ll, d), F32), jax.ShapeDtypeStruct((bn, ll, d), F32)],
        compiler_params=_cparams("arbitrary", "arbitrary"),
        name="rwkv_readout",
    )(y0, y1, rkv, kd, rkv, gate, ln_w.reshape(1, d), ln_b.reshape(1, d), r_k.reshape(1, d), wo_bf,
      _seg_ones(), x_all, mods, g4)


def _rope_tables(lat_len, ctx_len):
    m = HEAD_DIM // 4
    t = jnp.arange(lat_len)
    inv = ROPE_THETA ** (-jnp.arange(0, 2 * m, 2, dtype=F32) / (2 * m))
    ar = (t // GRID_W).astype(F32)[:, None] * inv[None, :]
    ac = (t % GRID_W).astype(F32)[:, None] * inv[None, :]
    cos = jnp.concatenate([jnp.cos(ar), jnp.cos(ar), jnp.cos(ac), jnp.cos(ac)], axis=-1)
    sin = jnp.concatenate([-jnp.sin(ar), jnp.sin(ar), -jnp.sin(ac), jnp.sin(ac)], axis=-1)
    cos = jnp.concatenate([cos, jnp.ones((ctx_len, HEAD_DIM), F32)], axis=0)
    sin = jnp.concatenate([sin, jnp.zeros((ctx_len, HEAD_DIM), F32)], axis=0)
    return cos, sin


def kernel(x, c, ctx, c_ctx, w_mod, b_mod, norm_g, ab_w_in, ab_q_gain, ab_k_gain, pool_w, pool_scale, ab_w_out, rw_mix, rw_wr, rw_wk, rw_wv, rw_wo, rw_w0, rw_w1, rw_w2, rw_a0, rw_a1, rw_a2, rw_g1, rw_g2, rw_k_k, rw_k_a, rw_r_k, rw_ln_w, rw_ln_b, router_w, router_b, moe_w_gate, moe_b_gate, moe_w_up, moe_b_up, moe_w_down, moe_b_down):
    bn, lat_len, d = x.shape
    ctx_len = ctx.shape[1]
    assert lat_len % TM == 0 and ctx_len == TM and d % LANES == 0
    n_lat_tiles = lat_len // TM
    tt = lat_len + ctx_len
    nt = tt // TM

    cvec = jnp.zeros((HALO, d), F32).at[:bn].set(c).at[bn].set(c_ctx)
    mods_all = _mod_rows(cvec, w_mod, b_mod)
    mods = [mods_all[i, :bn + 1].reshape(bn + 1, 6, d) for i in range(w_mod.shape[0])]

    x_all = jnp.concatenate([x, ctx], axis=1)

    cos_t, sin_t = _rope_tables(lat_len, ctx_len)
    q, k, v, pin = _inproj(x_all, mods[0], norm_g[0], ab_w_in[0].astype(BF16), ab_q_gain[0], ab_k_gain[0],
                           cos_t, sin_t, n_lat_tiles)
    o = _attention(q, k, v, n_lat_tiles)
    x_all, f = _mix0_out(o, pin, pool_w[0].astype(BF16), pool_scale[0], ab_w_out[0].astype(BF16), x_all,
                         mods[0], norm_g[0], n_lat_tiles, lat_len, ctx_len)
    tile_mod = jnp.where(jnp.arange(bn * nt) % nt < n_lat_tiles, jnp.arange(bn * nt) // nt, bn).astype(jnp.int32)
    x_all = _moe(f.reshape(bn * tt, d), x_all.reshape(bn * tt, d), tile_mod, mods[0], norm_g[0],
                 router_w[0], router_b[0], moe_w_gate[0], moe_b_gate[0], moe_w_up[0], moe_b_up[0],
                 moe_w_down[0], moe_b_down[0]).reshape(bn, tt, d)

    h = _normmod(x_all, mods[1], norm_g[1], n_lat_tiles)
    mix = rw_mix[0]
    w3 = jnp.stack([rw_wr[0], rw_wk[0], rw_wv[0]]).astype(BF16)
    rkv = _shiftproj(h, jnp.stack([mix[0], mix[2], mix[3]])[:, None, :], w3, n_lat_tiles)
    dec, kd, bb, nkk, gate = _lora(h, rkv, jnp.stack([mix[1], mix[4], mix[5]]), rw_w1[0], rw_w2[0], rw_a1[0],
                                   rw_a2[0], rw_g1[0], rw_g2[0], rw_w0[0], rw_a0[0], rw_k_k[0], rw_k_a[0],
                                   n_lat_tiles)
    ys = _scan(_to_chains(dec[0], dec[1], lat_len), _to_chains(kd[0], kd[1], lat_len),
               _to_chains(bb[0], bb[1], lat_len), _to_chains(nkk, nkk, lat_len),
               _to_chains(rkv[2], rkv[2], lat_len), _to_chains(rkv[0], rkv[0], lat_len))
    y0, y1 = _from_chains(ys, bn, lat_len, ctx_len)
    x_lat, f = _readout(y0, y1, rkv, kd, gate, rw_ln_w[0], rw_ln_b[0], rw_r_k[0].reshape(-1),
                        rw_wo[0].astype(BF16), x_all, mods[1], norm_g[1])
    tile_mod = (jnp.arange(bn * n_lat_tiles) // n_lat_tiles).astype(jnp.int32)
    out = _moe(f.reshape(bn * lat_len, d), x_lat.reshape(bn * lat_len, d), tile_mod, mods[1], norm_g[1],
               router_w[1], router_b[1], moe_w_gate[1], moe_b_gate[1], moe_w_up[1], moe_b_up[1],
               moe_w_down[1], moe_b_down[1])
    return out.reshape(bn, lat_len, d)
```

```python
import functools

import jax
import jax.numpy as jnp
from jax import lax
from jax.experimental import pallas as pl
from jax.experimental.pallas import tpu as pltpu

F32 = jnp.float32
BF16 = jnp.bfloat16

TM = 256
TL = 128
TP = 64
TE = 256
LANES = 128
HALO = 8
VMEM_LIMIT = 56 * 1024 * 1024

RMS_EPS = 1e-6
GN_EPS = 64e-5
HEAD_DIM = 128
N_HEADS = 8
N_KV_HEADS = 2
GQA_GROUP = N_HEADS // N_KV_HEADS
GRID_W = 64
ROPE_THETA = 10000.0
POOL_WINDOWS = (2, 4, 8, 16)
RWKV_HEAD = 64
LORA_PAD = 128
N_EXPERTS = 32
TOP_K = 4
SWIGLU_LIMIT = 7.0
SWIGLU_ALPHA = 1.702


def _cparams(*sem):
    return pltpu.CompilerParams(dimension_semantics=sem, vmem_limit_bytes=VMEM_LIMIT)


def _rms(x, g):
    return x * lax.rsqrt(jnp.mean(x * x, axis=-1, keepdims=True) + RMS_EPS) * g


def _sigmoid(x):
    return 1.0 / (1.0 + jnp.exp(-x))


def _dot(a, b):
    return jnp.dot(a, b, preferred_element_type=F32)


def _mod_kernel(c_ref, w_ref, b_ref, o_ref):
    c = c_ref[...]
    s = c * _sigmoid(c)
    o_ref[0] = _dot(s.astype(BF16), w_ref[0].astype(BF16)) + b_ref[0]


def _mod_rows(cvec, w_mod, b_mod):
    depth, d, n = w_mod.shape
    tn = 1024
    rows = cvec.shape[0]
    return pl.pallas_call(
        _mod_kernel,
        grid=(depth, n // tn),
        in_specs=[
            pl.BlockSpec((rows, d), lambda i, j: (0, 0)),
            pl.BlockSpec((1, d, tn), lambda i, j: (i, 0, j)),
            pl.BlockSpec((1, 1, tn), lambda i, j: (i, 0, j)),
        ],
        out_specs=pl.BlockSpec((1, rows, tn), lambda i, j: (i, 0, j)),
        out_shape=jax.ShapeDtypeStruct((depth, rows, n), F32),
        compiler_params=_cparams("arbitrary", "arbitrary"),
        name="mod_rows",
    )(cvec, w_mod, b_mod.reshape(depth, 1, n))


def _inproj_kernel(x_ref, mod_ref, g_ref, w_ref, qg_ref, kg_ref, cos_ref, sin_ref,
                   q_ref, k_ref, v_ref, pin_ref):
    x = x_ref[0]
    h = _rms(x, g_ref[0:1, :]) * (1.0 + mod_ref[0, 1:2, :]) + mod_ref[0, 0:1, :]
    u = _dot(h.astype(BF16), w_ref[...])
    cos = cos_ref[...]
    sin = sin_ref[...]
    lane = lax.broadcasted_iota(jnp.int32, (TM, HEAD_DIM), 1)
    first_half = (lane & 32) == 0

    def head(xh, gain, scale):
        y = _rms(xh, gain)
        sw = jnp.where(first_half, pltpu.roll(y, 96, 1), pltpu.roll(y, 32, 1))
        return ((y * cos + sw * sin) * scale).astype(BF16)

    attn_w = N_HEADS * HEAD_DIM
    kv_w = N_KV_HEADS * HEAD_DIM
    for hh in range(N_HEADS):
        sl = slice(hh * HEAD_DIM, (hh + 1) * HEAD_DIM)
        q_ref[0, :, sl] = head(u[:, sl], qg_ref[...], HEAD_DIM ** -0.5)
    for hh in range(N_KV_HEADS):
        sl = slice(hh * HEAD_DIM, (hh + 1) * HEAD_DIM)
        k_ref[0, :, sl] = head(u[:, attn_w + hh * HEAD_DIM:attn_w + (hh + 1) * HEAD_DIM], kg_ref[...], 1.0)
    v_ref[0] = u[:, attn_w + kv_w:attn_w + 2 * kv_w].astype(BF16)
    pin_ref[0] = u[:, attn_w + 2 * kv_w:]


def _inproj(x_all, mods, g4, w_in_bf, q_gain, k_gain, cos_t, sin_t, n_lat_tiles):
    bn, tt, d = x_all.shape
    nt = tt // TM
    attn_w = N_HEADS * HEAD_DIM
    kv_w = N_KV_HEADS * HEAD_DIM
    pool_w = w_in_bf.shape[1] - attn_w - 2 * kv_w
    mod_map = lambda b, t: (jnp.where(t < n_lat_tiles, b, bn), 0, 0)
    return pl.pallas_call(
        _inproj_kernel,
        grid=(bn, nt),
        in_specs=[
            pl.BlockSpec((1, TM, d), lambda b, t: (b, t, 0)),
            pl.BlockSpec((1, 6, d), mod_map),
            pl.BlockSpec((4, d), lambda b, t: (0, 0)),
            pl.BlockSpec(w_in_bf.shape, lambda b, t: (0, 0)),
            pl.BlockSpec((1, HEAD_DIM), lambda b, t: (0, 0)),
            pl.BlockSpec((1, HEAD_DIM), lambda b, t: (0, 0)),
            pl.BlockSpec((TM, HEAD_DIM), lambda b, t: (t, 0)),
            pl.BlockSpec((TM, HEAD_DIM), lambda b, t: (t, 0)),
        ],
        out_specs=[
            pl.BlockSpec((1, TM, attn_w), lambda b, t: (b, t, 0)),
            pl.BlockSpec((1, TM, kv_w), lambda b, t: (b, t, 0)),
            pl.BlockSpec((1, TM, kv_w), lambda b, t: (b, t, 0)),
            pl.BlockSpec((1, TM, pool_w), lambda b, t: (b, t, 0)),
        ],
        out_shape=[
            jax.ShapeDtypeStruct((bn, tt, attn_w), BF16),
            jax.ShapeDtypeStruct((bn, tt, kv_w), BF16),
            jax.ShapeDtypeStruct((bn, tt, kv_w), BF16),
            jax.ShapeDtypeStruct((bn, tt, pool_w), F32),
        ],
        compiler_params=_cparams("arbitrary", "arbitrary"),
        name="inproj",
    )(x_all, mods, g4, w_in_bf, q_gain.reshape(1, -1), k_gain.reshape(1, -1), cos_t, sin_t)


def _attn_kernel(q_ref, k_ref, v_ref, o_ref, *, n_lat_tiles, lat_len):
    t = pl.program_id(2)

    def attend(kk, vv):
        for i in range(GQA_GROUP):
            sl = slice(i * HEAD_DIM, (i + 1) * HEAD_DIM)
            s = lax.dot_general(q_ref[0, :, sl], kk, (((1,), (1,)), ((), ())), preferred_element_type=F32)
            p = jnp.exp(s - jnp.max(s, axis=-1, keepdims=True))
            l = jnp.sum(p, axis=-1, keepdims=True)
            o_ref[0, :, sl] = (_dot(p.astype(BF16), vv) / l).astype(BF16)

    @pl.when(t < n_lat_tiles)
    def _():
        attend(k_ref[0], v_ref[0])

    @pl.when(t >= n_lat_tiles)
    def _():
        attend(k_ref[0, lat_len:, :], v_ref[0, lat_len:, :])


def _attention(q, k, v, n_lat_tiles):
    bn, tt, attn_w = q.shape
    nt = tt // TM
    gw = GQA_GROUP * HEAD_DIM
    kern = functools.partial(_attn_kernel, n_lat_tiles=n_lat_tiles, lat_len=n_lat_tiles * TM)
    return pl.pallas_call(
        kern,
        grid=(bn, N_KV_HEADS, nt),
        in_specs=[
            pl.BlockSpec((1, TM, gw), lambda b, g, t: (b, t, g)),
            pl.BlockSpec((1, tt, HEAD_DIM), lambda b, g, t: (b, 0, g)),
            pl.BlockSpec((1, tt, HEAD_DIM), lambda b, g, t: (b, 0, g)),
        ],
        out_specs=pl.BlockSpec((1, TM, gw), lambda b, g, t: (b, t, g)),
        out_shape=jax.ShapeDtypeStruct((bn, tt, attn_w), BF16),
        compiler_params=_cparams("arbitrary", "arbitrary", "arbitrary"),
        name="attention",
    )(q, k, v)


def _post_mixer(y, x, mod_ref, g_ref):
    xn = x + mod_ref[0, 2:3, :] * _rms(y, g_ref[1:2, :])
    f = _rms(xn, g_ref[2:3, :]) * (1.0 + mod_ref[0, 4:5, :]) + mod_ref[0, 3:4, :]
    return xn, f


def _mix0_kernel(o_ref, pin_ref, pprev_ref, pnext_ref, pw_ref, ps_ref, wout_ref, x_ref, mod_ref, g_ref,
                 xo_ref, f_ref, ext_sc, cat_sc, *, n_lat_tiles, lat_len, ctx_len):
    t = pl.program_id(1)
    is_ctx = t >= n_lat_tiles
    seq_first = jnp.logical_or(t == 0, is_ctx)
    seq_last = jnp.logical_or(t == n_lat_tiles - 1, is_ctx)
    pin = pin_ref[0]
    ext_sc[0:HALO, :] = jnp.where(seq_first, 0.0, pprev_ref[0])
    ext_sc[HALO:HALO + TM, :] = pin
    ext_sc[HALO + TM:, :] = jnp.where(seq_last, 0.0, pnext_ref[0])
    row = lax.broadcasted_iota(jnp.int32, (TM, 1), 0)
    pos = jnp.where(is_ctx, 0, t * TM) + row
    n = jnp.where(is_ctx, ctx_len, lat_len)
    gw = pin.shape[1] // len(POOL_WINDOWS)
    attn_w = o_ref.shape[2]
    cat_sc[:, 0:attn_w] = o_ref[0]
    for gi, w in enumerate(POOL_WINDOWS):
        sl = slice(gi * gw, (gi + 1) * gw)
        base = HALO - w // 2
        acc = ext_sc[base:base + TM, sl]
        for j in range(1, w):
            acc = acc + ext_sc[base + j:base + j + TM, sl]
        cnt = (jnp.minimum(pos + w // 2, n) - jnp.maximum(pos - w // 2, 0)).astype(F32)
        m = acc / cnt - pin[:, sl]
        y = _dot(m.astype(BF16), pw_ref[gi]) * ps_ref[:, sl]
        cat_sc[:, attn_w + gi * gw:attn_w + (gi + 1) * gw] = y.astype(BF16)
    y = _dot(cat_sc[...], wout_ref[...])
    xn, f = _post_mixer(y, x_ref[0], mod_ref, g_ref)
    xo_ref[0] = xn
    f_ref[0] = f


def _mix0_out(o, pin, pool_w_bf, pool_scale, w_out_bf, x_all, mods, g4, n_lat_tiles, lat_len, ctx_len):
    bn, tt, d = x_all.shape
    nt = tt // TM
    attn_w = o.shape[2]
    pw = pin.shape[2]
    hb = TM // HALO
    n_hblk = tt // HALO
    mod_map = lambda b, t: (jnp.where(t < n_lat_tiles, b, bn), 0, 0)
    kern = functools.partial(_mix0_kernel, n_lat_tiles=n_lat_tiles, lat_len=lat_len, ctx_len=ctx_len)
    return pl.pallas_call(
        kern,
        grid=(bn, nt),
        in_specs=[
            pl.BlockSpec((1, TM, attn_w), lambda b, t: (b, t, 0)),
            pl.BlockSpec((1, TM, pw), lambda b, t: (b, t, 0)),
            pl.BlockSpec((1, HALO, pw), lambda b, t: (b, jnp.maximum(t * hb - 1, 0), 0)),
            pl.BlockSpec((1, HALO, pw), lambda b, t: (b, jnp.minimum((t + 1) * hb, n_hblk - 1), 0)),
            pl.BlockSpec(pool_w_bf.shape, lambda b, t: (0, 0, 0)),
            pl.BlockSpec((1, pw), lambda b, t: (0, 0)),
            pl.BlockSpec(w_out_bf.shape, lambda b, t: (0, 0)),
            pl.BlockSpec((1, TM, d), lambda b, t: (b, t, 0)),
            pl.BlockSpec((1, 6, d), mod_map),
            pl.BlockSpec((4, d), lambda b, t: (0, 0)),
        ],
        out_specs=[
            pl.BlockSpec((1, TM, d), lambda b, t: (b, t, 0)),
            pl.BlockSpec((1, TM, d), lambda b, t: (b, t, 0)),
        ],
        out_shape=[jax.ShapeDtypeStruct((bn, tt, d), F32), jax.ShapeDtypeStruct((bn, tt, d), F32)],
        scratch_shapes=[pltpu.VMEM((TM + 2 * HALO, pw), F32), pltpu.VMEM((TM, attn_w + pw), BF16)],
        compiler_params=_cparams("arbitrary", "arbitrary"),
        name="pool_outproj",
    )(o, pin, pin, pin, pool_w_bf, pool_scale.reshape(1, -1), w_out_bf, x_all, mods, g4)


def _router_kernel(f_ref, rw_ref, rb_ref, idx_ref, gate_ref, rank_ref, cnt_ref, carry_sc):
    i = pl.program_id(0)

    @pl.when(i == 0)
    def _():
        carry_sc[...] = jnp.zeros(carry_sc.shape, F32)

    lane = lax.broadcasted_iota(jnp.int32, (TM, LANES), 1).astype(F32)
    logits = _dot(f_ref[...].astype(BF16), rw_ref[...]) + rb_ref[...]
    l = jnp.where(lane < N_EXPERTS, logits, -jnp.inf)
    vals, sels, idxs = [], [], []
    for _ in range(TOP_K):
        m = jnp.max(l, axis=-1, keepdims=True)
        idx = jnp.min(jnp.where(l == m, lane, float(LANES)), axis=-1, keepdims=True)
        sel = lane == idx
        vals.append(m)
        idxs.append(idx)
        sels.append(sel)
        l = jnp.where(sel, -jnp.inf, l)
    es = [jnp.exp(v - vals[0]) for v in vals]
    denom = es[0] + es[1] + es[2] + es[3]
    onehot = jnp.zeros((TM, LANES), F32)
    for sel in sels:
        onehot = onehot + jnp.where(sel, 1.0, 0.0)
    r_i = lax.broadcasted_iota(jnp.int32, (TM, TM), 0)
    c_i = lax.broadcasted_iota(jnp.int32, (TM, TM), 1)
    tril = jnp.where(c_i < r_i, 1.0, 0.0).astype(BF16)
    before = _dot(tril, onehot.astype(BF16)) + carry_sc[0:1, :]
    idx_out = jnp.zeros((TM, LANES), F32)
    gate_out = jnp.zeros((TM, LANES), F32)
    rank_out = jnp.zeros((TM, LANES), F32)
    for k in range(TOP_K):
        rk = jnp.sum(jnp.where(sels[k], before, 0.0), axis=-1, keepdims=True)
        idx_out = jnp.where(lane == k, idxs[k], idx_out)
        gate_out = jnp.where(lane == k, es[k] / denom, gate_out)
        rank_out = jnp.where(lane == k, rk, rank_out)
    idx_ref[...] = idx_out.astype(jnp.int32)
    gate_ref[...] = gate_out
    rank_ref[...] = rank_out.astype(jnp.int32)
    carry_sc[...] = carry_sc[...] + jnp.sum(onehot, axis=0, keepdims=True)
    cnt_ref[...] = carry_sc[...].astype(jnp.int32)


def _router(f2, router_w, router_b):
    n, d = f2.shape
    rw = jnp.zeros((d, LANES), BF16).at[:, :N_EXPERTS].set(router_w.astype(BF16))
    rb = jnp.zeros((1, LANES), F32).at[0, :N_EXPERTS].set(router_b)
    tile = pl.BlockSpec((TM, LANES), lambda i: (i, 0))
    return pl.pallas_call(
        _router_kernel,
        grid=(n // TM,),
        in_specs=[
            pl.BlockSpec((TM, d), lambda i: (i, 0)),
            pl.BlockSpec((d, LANES), lambda i: (0, 0)),
            pl.BlockSpec((1, LANES), lambda i: (0, 0)),
        ],
        out_specs=[tile, tile, tile, pl.BlockSpec((HALO, LANES), lambda i: (0, 0))],
        out_shape=[
            jax.ShapeDtypeStruct((n, LANES), jnp.int32),
            jax.ShapeDtypeStruct((n, LANES), F32),
            jax.ShapeDtypeStruct((n, LANES), jnp.int32),
            jax.ShapeDtypeStruct((HALO, LANES), jnp.int32),
        ],
        scratch_shapes=[pltpu.VMEM((HALO, LANES), F32)],
        compiler_params=_cparams("arbitrary"),
        name="router",
    )(f2, rw, rb)


def _expert_kernel(be_ref, tok_ref, nb_ref, f_hbm, wg_ref, bg_ref, wu_ref, bu_ref, wd_ref, bd_ref,
                   y_ref, xbuf, sem):
    s = pl.program_id(0)
    nb = nb_ref[0]

    def issue(blk, slot):
        base = blk * TE

        def body(r, carry):
            tok = tok_ref[base + r]
            pltpu.make_async_copy(f_hbm.at[pl.ds(tok, 1)], xbuf.at[slot, pl.ds(r, 1)], sem.at[slot]).start()
            return carry

        lax.fori_loop(0, TE, body, 0)

    @pl.when(s == 0)
    def _():
        issue(0, 0)

    @pl.when(s + 1 < nb)
    def _():
        issue(s + 1, (s + 1) % 2)

    @pl.when(s < nb)
    def _():
        slot = s % 2
        pltpu.make_async_copy(f_hbm.at[pl.ds(0, TE)], xbuf.at[slot], sem.at[slot]).wait()
        x = xbuf[slot].astype(BF16)
        glu = jnp.minimum(_dot(x, wg_ref[0]) + bg_ref[0], SWIGLU_LIMIT)
        lin = jnp.clip(_dot(x, wu_ref[0]) + bu_ref[0], -SWIGLU_LIMIT, SWIGLU_LIMIT)
        act = glu * _sigmoid(SWIGLU_ALPHA * glu) * (lin + 1.0)
        y_ref[...] = _dot(act.astype(BF16), wd_ref[0]) + bd_ref[0]

    @pl.when(s >= nb)
    def _():
        y_ref[...] = jnp.zeros(y_ref.shape, F32)


def _experts(f2, block_e, slot_tok, n_used, wg, bg, wu, bu, wd, bd, n_blocks):
    n, d = f2.shape
    ne, _, de = wg.shape
    grid_spec = pltpu.PrefetchScalarGridSpec(
        num_scalar_prefetch=3,
        grid=(n_blocks,),
        in_specs=[
            pl.BlockSpec(memory_space=pl.ANY),
            pl.BlockSpec((1, d, de), lambda s, be, tok, nb: (be[s], 0, 0)),
            pl.BlockSpec((1, 1, de), lambda s, be, tok, nb: (be[s], 0, 0)),
            pl.BlockSpec((1, d, de), lambda s, be, tok, nb: (be[s], 0, 0)),
            pl.BlockSpec((1, 1, de), lambda s, be, tok, nb: (be[s], 0, 0)),
            pl.BlockSpec((1, de, d), lambda s, be, tok, nb: (be[s], 0, 0)),
            pl.BlockSpec((1, 1, d), lambda s, be, tok, nb: (be[s], 0, 0)),
        ],
        out_specs=pl.BlockSpec((TE, d), lambda s, be, tok, nb: (s, 0)),
        scratch_shapes=[pltpu.VMEM((2, TE, d), F32), pltpu.SemaphoreType.DMA((2,))],
    )
    return pl.pallas_call(
        _expert_kernel,
        grid_spec=grid_spec,
        out_shape=jax.ShapeDtypeStruct((n_blocks * TE, d), F32),
        compiler_params=_cparams("arbitrary"),
        name="experts",
    )(block_e, slot_tok, n_used, f2, wg, bg.reshape(ne, 1, de), wu, bu.reshape(ne, 1, de), wd, bd.reshape(ne, 1, d))


def _combine_kernel(dest_ref, tmod_ref, yb_hbm, gate_ref, x_ref, mod_ref, g_ref, xo_ref, buf, sem):
    i = pl.program_id(0)
    nt = pl.num_programs(0)

    def issue(tile, slot):
        base = tile * TM * TOP_K

        def body(r, carry):
            for k in range(TOP_K):
                dst = dest_ref[base + r * TOP_K + k]
                pltpu.make_async_copy(yb_hbm.at[pl.ds(dst, 1)], buf.at[slot, k, pl.ds(r, 1)], sem.at[slot]).start()
            return carry

        lax.fori_loop(0, TM, body, 0)

    @pl.when(i == 0)
    def _():
        issue(0, 0)

    @pl.when(i + 1 < nt)
    def _():
        issue(i + 1, (i + 1) % 2)

    slot = i % 2
    for k in range(TOP_K):
        pltpu.make_async_copy(yb_hbm.at[pl.ds(0, TM)], buf.at[slot, k], sem.at[slot]).wait()
    gate = gate_ref[...]
    out = gate[:, 0:1] * buf[slot, 0]
    for k in range(1, TOP_K):
        out = out + gate[:, k:k + 1] * buf[slot, k]
    xo_ref[...] = x_ref[...] + mod_ref[0, 5:6, :] * _rms(out, g_ref[3:4, :])


def _combine(yb, dest, tile_mod, gates, x2, mods, g4):
    n, d = x2.shape
    grid_spec = pltpu.PrefetchScalarGridSpec(
        num_scalar_prefetch=2,
        grid=(n // TM,),
        in_specs=[
            pl.BlockSpec(memory_space=pl.ANY),
            pl.BlockSpec((TM, LANES), lambda i, dst, tm: (i, 0)),
            pl.BlockSpec((TM, d), lambda i, dst, tm: (i, 0)),
            pl.BlockSpec((1, 6, d), lambda i, dst, tm: (tm[i], 0, 0)),
            pl.BlockSpec((4, d), lambda i, dst, tm: (0, 0)),
        ],
        out_specs=pl.BlockSpec((TM, d), lambda i, dst, tm: (i, 0)),
        scratch_shapes=[pltpu.VMEM((2, TOP_K, TM, d), F32), pltpu.SemaphoreType.DMA((2,))],
    )
    return pl.pallas_call(
        _combine_kernel,
        grid_spec=grid_spec,
        out_shape=jax.ShapeDtypeStruct((n, d), F32),
        compiler_params=_cparams("arbitrary"),
        name="moe_combine",
    )(dest, tile_mod, yb, gates, x2, mods, g4)


def _moe(f2, x2, tile_mod, mods, g4, router_w, router_b, wg, bg, wu, bu, wd, bd):
    n, d = f2.shape
    idx, gates, rank, cnt = _router(f2, router_w, router_b)
    idx = idx[:, :TOP_K]
    rank = rank[:, :TOP_K]
    counts = cnt[0, :N_EXPERTS]
    padded = (counts + TE - 1) // TE * TE
    pend = jnp.cumsum(padded)
    pstart = pend - padded
    dest = (pstart[idx] + rank).reshape(-1).astype(jnp.int32)
    n_asg = n * TOP_K
    n_blocks = -(-(n_asg + N_EXPERTS * (TE - 1)) // TE)
    n_used = (pend[-1] // TE).astype(jnp.int32)
    blk = jnp.arange(n_blocks, dtype=jnp.int32)
    block_e = jnp.searchsorted(pend, jnp.minimum(blk, n_used - 1) * TE, side="right").astype(jnp.int32)
    block_e = jnp.minimum(block_e, N_EXPERTS - 1)
    slot_tok = jnp.zeros((n_blocks * TE,), jnp.int32).at[dest].set(jnp.repeat(jnp.arange(n, dtype=jnp.int32), TOP_K))
    yb = _experts(f2, block_e, slot_tok, n_used.reshape(1), wg.astype(BF16), bg, wu.astype(BF16), bu,
                  wd.astype(BF16), bd, n_blocks)
    return _combine(yb, dest, tile_mod, gates, x2, mods, g4)


def _normmod_kernel(x_ref, mod_ref, g_ref, h_ref):
    h_ref[0] = _rms(x_ref[0], g_ref[0:1, :]) * (1.0 + mod_ref[0, 1:2, :]) + mod_ref[0, 0:1, :]


def _normmod(x_all, mods, g4, n_lat_tiles):
    bn, tt, d = x_all.shape
    mod_map = lambda b, t: (jnp.where(t < n_lat_tiles, b, bn), 0, 0)
    tile = pl.BlockSpec((1, TM, d), lambda b, t: (b, t, 0))
    return pl.pallas_call(
        _normmod_kernel,
        grid=(bn, tt // TM),
        in_specs=[tile, pl.BlockSpec((1, 6, d), mod_map), pl.BlockSpec((4, d), lambda b, t: (0, 0))],
        out_specs=tile,
        out_shape=jax.ShapeDtypeStruct((bn, tt, d), F32),
        compiler_params=_cparams("arbitrary", "arbitrary"),
        name="norm_mod",
    )(x_all, mods, g4)


def _centred_shift(h, hprev_ref, hnext_ref, t, n_lat_tiles, n_tiles):
    seq_first = jnp.logical_or(t == 0, t == n_lat_tiles)
    seq_last = jnp.logical_or(t == n_lat_tiles - 1, t == n_tiles - 1)
    prev_row = jnp.where(seq_first, 0.0, hprev_ref[0, HALO - 1:HALO, :])
    next_row = jnp.where(seq_last, 0.0, hnext_ref[0, 0:1, :])
    tm = h.shape[0]
    row = lax.broadcasted_iota(jnp.int32, (tm, 1), 0)
    prev = jnp.where(row == 0, prev_row, pltpu.roll(h, 1, 0))
    nxt = jnp.where(row == tm - 1, next_row, pltpu.roll(h, tm - 1, 0))
    return 0.5 * (prev + nxt) - h


def _halo_specs(d, tt, tm):
    hb = tm // HALO
    n_hblk = tt // HALO
    return [
        pl.BlockSpec((1, tm, d), lambda *a: (a[-2], a[-1], 0)),
        pl.BlockSpec((1, HALO, d), lambda *a: (a[-2], jnp.maximum(a[-1] * hb - 1, 0), 0)),
        pl.BlockSpec((1, HALO, d), lambda *a: (a[-2], jnp.minimum((a[-1] + 1) * hb, n_hblk - 1), 0)),
    ]


def _shiftproj_kernel(h_ref, hprev_ref, hnext_ref, mix_ref, w_ref, o_ref, *, n_lat_tiles):
    t = pl.program_id(2)
    h = h_ref[0]
    xx = _centred_shift(h, hprev_ref, hnext_ref, t, n_lat_tiles, pl.num_programs(2))
    o_ref[0, 0] =_dot((h + xx * mix_ref[0]).astype(BF16), w_ref[0])


def _shiftproj(h, mix3, w3_bf, n_lat_tiles):
    bn, tt, d = h.shape
    nj = w3_bf.shape[0]
    kern = functools.partial(_shiftproj_kernel, n_lat_tiles=n_lat_tiles)
    return pl.pallas_call(
        kern,
        grid=(nj, bn, tt // TM),
        in_specs=_halo_specs(d, tt, TM) + [
            pl.BlockSpec((1, 1, d), lambda j, b, t: (j, 0, 0)),
            pl.BlockSpec((1, d, d), lambda j, b, t: (j, 0, 0)),
        ],
        out_specs=pl.BlockSpec((1, 1, TM, d), lambda j, b, t: (j, b, t, 0)),
        out_shape=jax.ShapeDtypeStruct((nj, bn, tt, d), F32),
        compiler_params=_cparams("arbitrary", "arbitrary", "arbitrary"),
        name="shift_proj",
    )(h, h, h, mix3, w3_bf)


def _softplus(x):
    return jnp.maximum(x, 0.0) + jnp.log(1.0 + jnp.exp(-jnp.abs(x)))


def _lora_kernel(h_ref, hprev_ref, hnext_ref, mix_ref, w1_ref, a1_ref, g1_ref, w2_ref, a2_ref, g2_ref,
                 w0_ref, a0_ref, dec_ref, aa_ref, gate_ref, *, n_lat_tiles):
    t = pl.program_id(1)
    h = h_ref[0]
    xx = _centred_shift(h, hprev_ref, hnext_ref, t, n_lat_tiles, pl.num_programs(1))
    hw = jnp.tanh(_dot((h + xx * mix_ref[0:1, :]).astype(BF16), w1_ref[...]))
    ha = _dot((h + xx * mix_ref[1:2, :]).astype(BF16), a1_ref[...])
    hg = _sigmoid(_dot((h + xx * mix_ref[2:3, :]).astype(BF16), g1_ref[...]))
    gate_ref[0] = _dot(hg.astype(BF16), g2_ref[...])
    for dd in range(2):
        sl = slice(dd * LORA_PAD, (dd + 1) * LORA_PAD)
        z = w0_ref[dd] + _dot(hw[:, sl].astype(BF16), w2_ref[dd])
        w_log = -_softplus(-z) - 0.5
        dec_ref[dd, 0] = jnp.exp(-jnp.exp(w_log))
        aa_ref[dd, 0] = _sigmoid(a0_ref[dd] + _dot(ha[:, sl].astype(BF16), a2_ref[dd]))


def _pad_lora(w_in, w_out):
    nd, d, r = w_in.shape
    wi = jnp.zeros((d, nd * LORA_PAD), BF16)
    wo = jnp.zeros((nd, LORA_PAD, d), BF16)
    for dd in range(nd):
        wi = wi.at[:, dd * LORA_PAD:dd * LORA_PAD + r].set(w_in[dd].astype(BF16))
        wo = wo.at[dd, :r].set(w_out[dd].astype(BF16))
    return wi, wo


def _lora(h, mix3, w1, w2, a1, a2, g1, g2, w0, a0, n_lat_tiles):
    bn, tt, d = h.shape
    w1p, w2p = _pad_lora(w1, w2)
    a1p, a2p = _pad_lora(a1, a2)
    const2 = lambda s: pl.BlockSpec(s, lambda b, t: (0, 0))
    const3 = lambda s: pl.BlockSpec(s, lambda b, t: (0, 0, 0))
    kern = functools.partial(_lora_kernel, n_lat_tiles=n_lat_tiles * (TM // TL))
    dir_out = pl.BlockSpec((2, 1, TL, d), lambda b, t: (0, b, t, 0))
    tok_out = pl.BlockSpec((1, TL, d), lambda b, t: (b, t, 0))
    dir_shape = jax.ShapeDtypeStruct((2, bn, tt, d), F32)
    tok_shape = jax.ShapeDtypeStruct((bn, tt, d), F32)
    return pl.pallas_call(
        kern,
        grid=(bn, tt // TL),
        in_specs=_halo_specs(d, tt, TL) + [
            const2((3, d)), const2(w1p.shape), const2(a1p.shape), const2((d, g1.shape[1])),
            const3(w2p.shape), const3(a2p.shape), const2(g2.shape),
            const3((2, 1, d)), const3((2, 1, d)),
        ],
        out_specs=[dir_out, dir_out, tok_out],
        out_shape=[dir_shape, dir_shape, tok_shape],
        compiler_params=_cparams("arbitrary", "arbitrary"),
        name="rwkv_lora",
    )(h, h, h, mix3, w1p, a1p, g1.astype(BF16), w2p, a2p, g2.astype(BF16),
      w0.reshape(2, 1, d), a0.reshape(2, 1, d))


N_CH = RWKV_HEAD
N_RH = 32
N_GRP = LANES // N_RH


def _lane_group(rows):
    return lax.broadcasted_iota(jnp.int32, (rows, LANES), 1) // N_RH


def _pick_groups(pieces, lg):
    out = pieces[N_GRP - 1]
    for g in range(N_GRP - 2, -1, -1):
        out = jnp.where(lg == g, pieces[g], out)
    return out


def _to_chain_block(srcs, c, lg):
    j, i = divmod(c, N_GRP)
    pieces = []
    for g in range(N_GRP):
        blk = srcs[g][:, j * LANES:(j + 1) * LANES]
        shift = ((g - i) % N_GRP) * N_RH
        pieces.append(blk if shift == 0 else pltpu.roll(blk, shift, 1))
    return _pick_groups(pieces, lg)


def _head_sum(x):
    s = x[:, 0:LANES]
    for j in range(1, x.shape[1] // LANES):
        s = s + x[:, j * LANES:(j + 1) * LANES]
    s = s + pltpu.roll(s, 2 * N_RH, 1)
    return s + pltpu.roll(s, N_RH, 1)


def _prep_kernel(rkv_ref, dec_ref, aa_ref, kk_ref, ka_ref, w_ref, kd_ref, bb_ref, nk_ref, v_ref, r_ref):
    rows = rkv_ref.shape[2]
    bn = rkv_ref.shape[1]
    lg = _lane_group(rows)
    nblk = kk_ref.shape[1] // LANES
    ks, kks = [], []
    for b in range(bn):
        k = rkv_ref[1, b]
        kk = k * kk_ref[...]
        inv = lax.rsqrt(jnp.maximum(_head_sum(kk * kk), 1e-24))
        ks.append(k)
        kks.append(kk * jnp.concatenate([inv] * nblk, axis=1))
    groups = [(dd, b) for dd in range(2) for b in range(bn)]
    kd_src = [ks[b] * (1.0 + (aa_ref[dd, b] - 1.0) * ka_ref[...]) for dd, b in groups]
    bb_src = [kks[b] * aa_ref[dd, b] for dd, b in groups]
    w_src = [dec_ref[dd, b] for dd, b in groups]
    nk_src = [-kks[b] for _, b in groups]
    v_src = [rkv_ref[2, b] for _, b in groups]
    r_src = [rkv_ref[0, b] for _, b in groups]
    for c in range(N_CH):
        sl = slice(c * LANES, (c + 1) * LANES)
        w_ref[:, sl] = _to_chain_block(w_src, c, lg)
        kd_ref[:, sl] = _to_chain_block(kd_src, c, lg)
        bb_ref[:, sl] = _to_chain_block(bb_src, c, lg)
        nk_ref[:, sl] = _to_chain_block(nk_src, c, lg)
        v_ref[:, sl] = _to_chain_block(v_src, c, lg)
        r_ref[:, sl] = _to_chain_block(r_src, c, lg)


def _prep(rkv, dec, aa, k_k, k_a):
    _, bn, tt, d = rkv.shape
    assert bn * 2 == N_GRP and d == N_CH * N_RH
    out = pl.BlockSpec((TP, N_CH * LANES), lambda t: (t, 0))
    shape = jax.ShapeDtypeStruct((tt, N_CH * LANES), F32)
    return pl.pallas_call(
        _prep_kernel,
        grid=(tt // TP,),
        in_specs=[
            pl.BlockSpec((3, bn, TP, d), lambda t: (0, 0, t, 0)),
            pl.BlockSpec((2, bn, TP, d), lambda t: (0, 0, t, 0)),
            pl.BlockSpec((2, bn, TP, d), lambda t: (0, 0, t, 0)),
            pl.BlockSpec((1, d), lambda t: (0, 0)),
            pl.BlockSpec((1, d), lambda t: (0, 0)),
        ],
        out_specs=[out] * 6,
        out_shape=[shape] * 6,
        compiler_params=_cparams("arbitrary"),
        name="rwkv_prep",
    )(rkv, dec, aa, k_k.reshape(1, d), k_a.reshape(1, d))


def _scan_kernel(*refs, steps):
    fwd_refs, rev_refs = refs[0:6], refs[6:12]
    yf_ref, yr_ref, s_sc = refs[12:15]

    @pl.when(pl.program_id(0) == 0)
    def _():
        s_sc[...] = jnp.zeros(s_sc.shape, F32)

    is_fwd = lax.broadcasted_iota(jnp.int32, (1, LANES), 1) < LANES // 2

    def step(i, carry):
        ir = steps - 1 - i
        w, k, b, a, v, r = (jnp.where(is_fwd, f[i], g[ir]) for f, g in zip(fwd_refs, rev_refs))
        wr = w * r
        br = jnp.sum(b * r, axis=0, keepdims=True)
        kr = jnp.sum(k * r, axis=0, keepdims=True)
        for vi in range(RWKV_HEAD):
            s = s_sc[vi]
            sa = jnp.sum(s * a, axis=0, keepdims=True)
            y0 = jnp.sum(s * wr, axis=0, keepdims=True)
            vv = v[vi:vi + 1, :]
            s_sc[vi] = s * w + sa * b + vv * k
            y = y0 + sa * br + vv * kr
            yf_ref[i, vi:vi + 1, :] = y
            yr_ref[ir, vi:vi + 1, :] = y
        return carry

    lax.fori_loop(0, steps, step, 0)


def _scan(ops, lat_len, ctx_len, steps=32):
    tt = ops[0].shape[0]
    n_lat, n_ctx = lat_len // steps, ctx_len // steps
    ops = [x.reshape(tt, N_CH, LANES) for x in ops]

    def fwd_blk(i):
        return jnp.where(i < n_ctx, n_lat + i, i - n_ctx)

    def rev_blk(i):
        return jnp.where(i < n_ctx, n_lat + n_ctx - 1 - i, n_lat - 1 - (i - n_ctx))

    fwd = pl.BlockSpec((steps, N_CH, LANES), lambda i: (fwd_blk(i), 0, 0))
    rev = pl.BlockSpec((steps, N_CH, LANES), lambda i: (rev_blk(i), 0, 0))
    shape = jax.ShapeDtypeStruct((tt, N_CH, LANES), F32)
    return pl.pallas_call(
        functools.partial(_scan_kernel, steps=steps),
        grid=(tt // steps,),
        in_specs=[fwd] * 6 + [rev] * 6,
        out_specs=[fwd, rev],
        out_shape=[shape, shape],
        scratch_shapes=[pltpu.VMEM((N_CH, N_CH, LANES), F32)],
        compiler_params=_cparams("arbitrary"),
        name="rwkv_scan",
    )(*ops, *ops)


def _readout_kernel(yf_ref, yr_ref, r_ref, kd_ref, v_ref, gate_ref, lnw_ref, lnb_ref, rk_ref, wo_ref,
                    x_ref, mod_ref, g_ref, xo_ref, f_ref, o_sc):
    rows = yf_ref.shape[0]
    bn = x_ref.shape[0]
    half = LANES // 2
    lg = _lane_group(rows)
    blk = lambda ref, c: ref[:, c * LANES:(c + 1) * LANES]
    ys = [blk(yf_ref, c) + pltpu.roll(blk(yr_ref, c), half, 1) for c in range(N_CH)]
    inv_n = 1.0 / N_CH
    mu = ys[0]
    for c in range(1, N_CH):
        mu = mu + ys[c]
    mu = mu * inv_n
    var = (ys[0] - mu) * (ys[0] - mu)
    for c in range(1, N_CH):
        var = var + (ys[c] - mu) * (ys[c] - mu)
    rstd = lax.rsqrt(var * inv_n + GN_EPS)
    s = blk(r_ref, 0) * blk(rk_ref, 0) * blk(kd_ref, 0)
    for c in range(1, N_CH):
        s = s + blk(r_ref, c) * blk(rk_ref, c) * blk(kd_ref, c)
    s = s + pltpu.roll(s, half, 1)
    for c in range(N_CH):
        o_sc[:, c * LANES:(c + 1) * LANES] = ((ys[c] - mu) * rstd * blk(lnw_ref, c) + blk(lnb_ref, c)
                                              + s * blk(v_ref, c))
    for b in range(bn):
        cols = []
        for j in range(N_CH // N_GRP):
            pieces = []
            for i in range(N_GRP):
                c = j * N_GRP + i
                shift = ((i - b) % N_GRP) * N_RH
                piece = o_sc[:, c * LANES:(c + 1) * LANES]
                pieces.append(piece if shift == 0 else pltpu.roll(piece, shift, 1))
            cols.append(_pick_groups(pieces, lg))
        o = jnp.concatenate(cols, axis=1) * gate_ref[b]
        yo = _dot(o.astype(BF16), wo_ref[...])
        xn = x_ref[b] + mod_ref[b, 2:3, :] * _rms(yo, g_ref[1:2, :])
        xo_ref[b] = xn
        f_ref[b] = _rms(xn, g_ref[2:3, :]) * (1.0 + mod_ref[b, 4:5, :]) + mod_ref[b, 3:4, :]


def _chain_param(p):
    pc = p.reshape(N_RH, N_CH).T
    return jnp.tile(pc, (1, N_GRP)).reshape(1, N_CH * LANES)


def _readout(yf, yr, r_c, kd_c, v_c, gate, ln_w, ln_b, r_k, wo_bf, x_all, mods, g4, lat_len):
    bn, tt, d = x_all.shape
    cw = N_CH * LANES
    chain = pl.BlockSpec((TP, cw), lambda t: (t, 0))
    tok = pl.BlockSpec((bn, TP, d), lambda t: (0, t, 0))
    row = pl.BlockSpec((1, cw), lambda t: (0, 0))
    out_shape = jax.ShapeDtypeStruct((bn, lat_len, d), F32)
    return pl.pallas_call(
        _readout_kernel,
        grid=(lat_len // TP,),
        in_specs=[
            chain, chain, chain, chain, chain, tok, row, row, row,
            pl.BlockSpec((d, d), lambda t: (0, 0)),
            tok,
            pl.BlockSpec((bn + 1, 6, d), lambda t: (0, 0, 0)),
            pl.BlockSpec((4, d), lambda t: (0, 0)),
        ],
        out_specs=[tok, tok],
        out_shape=[out_shape, out_shape],
        scratch_shapes=[pltpu.VMEM((TP, cw), F32)],
        compiler_params=_cparams("arbitrary"),
        name="rwkv_readout",
    )(yf.reshape(tt, cw), yr.reshape(tt, cw), r_c, kd_c, v_c, gate, _chain_param(ln_w), _chain_param(ln_b),
      _chain_param(r_k), wo_bf, x_all, mods, g4)


def _rope_tables(lat_len, ctx_len):
    m = HEAD_DIM // 4
    t = jnp.arange(lat_len)
    inv = ROPE_THETA ** (-jnp.arange(0, 2 * m, 2, dtype=F32) / (2 * m))
    ar = (t // GRID_W).astype(F32)[:, None] * inv[None, :]
    ac = (t % GRID_W).astype(F32)[:, None] * inv[None, :]
    cos = jnp.concatenate([jnp.cos(ar), jnp.cos(ar), jnp.cos(ac), jnp.cos(ac)], axis=-1)
    sin = jnp.concatenate([-jnp.sin(ar), jnp.sin(ar), -jnp.sin(ac), jnp.sin(ac)], axis=-1)
    cos = jnp.concatenate([cos, jnp.ones((ctx_len, HEAD_DIM), F32)], axis=0)
    sin = jnp.concatenate([sin, jnp.zeros((ctx_len, HEAD_DIM), F32)], axis=0)
    return cos, sin


def kernel(x, c, ctx, c_ctx, w_mod, b_mod, norm_g, ab_w_in, ab_q_gain, ab_k_gain, pool_w, pool_scale, ab_w_out, rw_mix, rw_wr, rw_wk, rw_wv, rw_wo, rw_w0, rw_w1, rw_w2, rw_a0, rw_a1, rw_a2, rw_g1, rw_g2, rw_k_k, rw_k_a, rw_r_k, rw_ln_w, rw_ln_b, router_w, router_b, moe_w_gate, moe_b_gate, moe_w_up, moe_b_up, moe_w_down, moe_b_down):
    bn, lat_len, d = x.shape
    ctx_len = ctx.shape[1]
    assert lat_len % TM == 0 and ctx_len == TM and d % LANES == 0
    n_lat_tiles = lat_len // TM
    tt = lat_len + ctx_len
    nt = tt // TM

    cvec = jnp.zeros((HALO, d), F32).at[:bn].set(c).at[bn].set(c_ctx)
    mods_all = _mod_rows(cvec, w_mod, b_mod)
    mods = [mods_all[i, :bn + 1].reshape(bn + 1, 6, d) for i in range(w_mod.shape[0])]

    x_all = jnp.concatenate([x, ctx], axis=1)

    cos_t, sin_t = _rope_tables(lat_len, ctx_len)
    q, k, v, pin = _inproj(x_all, mods[0], norm_g[0], ab_w_in[0].astype(BF16), ab_q_gain[0], ab_k_gain[0],
                           cos_t, sin_t, n_lat_tiles)
    o = _attention(q, k, v, n_lat_tiles)
    x_all, f = _mix0_out(o, pin, pool_w[0].astype(BF16), pool_scale[0], ab_w_out[0].astype(BF16), x_all,
                         mods[0], norm_g[0], n_lat_tiles, lat_len, ctx_len)
    tile_mod = jnp.where(jnp.arange(bn * nt) % nt < n_lat_tiles, jnp.arange(bn * nt) // nt, bn).astype(jnp.int32)
    x_all = _moe(f.reshape(bn * tt, d), x_all.reshape(bn * tt, d), tile_mod, mods[0], norm_g[0],
                 router_w[0], router_b[0], moe_w_gate[0], moe_b_gate[0], moe_w_up[0], moe_b_up[0],
                 moe_w_down[0], moe_b_down[0]).reshape(bn, tt, d)

    h = _normmod(x_all, mods[1], norm_g[1], n_lat_tiles)
    mix = rw_mix[0]
    perm = jnp.arange(d).reshape(N_RH, N_CH).T.reshape(-1)
    w3 = jnp.stack([rw_wr[0], rw_wk[0], rw_wv[0]])[:, :, perm].astype(BF16)
    rkv = _shiftproj(h, jnp.stack([mix[0], mix[2], mix[3]])[:, None, :], w3, n_lat_tiles)
    dec, aa, gate = _lora(h, jnp.stack([mix[1], mix[4], mix[5]]), rw_w1[0], rw_w2[0][:, :, perm], rw_a1[0],
                          rw_a2[0][:, :, perm], rw_g1[0], rw_g2[0][:, perm], rw_w0[0][:, perm], rw_a0[0][:, perm],
                          n_lat_tiles)
    w_c, kd_c, bb_c, nk_c, v_c, r_c = _prep(rkv, dec, aa, rw_k_k[0][perm], rw_k_a[0][perm])
    yf, yr = _scan([w_c, kd_c, bb_c, nk_c, v_c, r_c], lat_len, ctx_len)
    x_lat, f = _readout(yf, yr, r_c, kd_c, v_c, gate, rw_ln_w[0], rw_ln_b[0], rw_r_k[0].reshape(-1),
                        rw_wo[0][perm, :].astype(BF16), x_all, mods[1], norm_g[1], lat_len)
    tile_mod = (jnp.arange(bn * n_lat_tiles) // n_lat_tiles).astype(jnp.int32)
    out = _moe(f.reshape(bn * lat_len, d), x_lat.reshape(bn * lat_len, d), tile_mod, mods[1], norm_g[1],
               router_w[1], router_b[1], moe_w_gate[1], moe_b_gate[1], moe_w_up[1], moe_b_up[1],
               moe_w_down[1], moe_b_down[1])
    return out.reshape(bn, lat_len, d)
```

```python
import functools

import jax
import jax.numpy as jnp
from jax import lax
from jax.experimental import pallas as pl
from jax.experimental.pallas import tpu as pltpu

F32 = jnp.float32
BF16 = jnp.bfloat16

TM = 256
TL = 128
TP = 64
TE = 256
SCAN_UNROLL = 16
DMA_UNROLL = 8
LANES = 128
HALO = 8
VMEM_LIMIT = 56 * 1024 * 1024

RMS_EPS = 1e-6
GN_EPS = 64e-5
HEAD_DIM = 128
N_HEADS = 8
N_KV_HEADS = 2
GQA_GROUP = N_HEADS // N_KV_HEADS
GRID_W = 64
ROPE_THETA = 10000.0
POOL_WINDOWS = (2, 4, 8, 16)
RWKV_HEAD = 64
LORA_PAD = 128
N_EXPERTS = 32
TOP_K = 4
SWIGLU_LIMIT = 7.0
SWIGLU_ALPHA = 1.702


def _cparams(*sem):
    return pltpu.CompilerParams(dimension_semantics=sem, vmem_limit_bytes=VMEM_LIMIT)


def _rms(x, g):
    return x * lax.rsqrt(jnp.mean(x * x, axis=-1, keepdims=True) + RMS_EPS) * g


def _sigmoid(x):
    return 1.0 / (1.0 + jnp.exp(-x))


def _dot(a, b):
    return jnp.dot(a, b, preferred_element_type=F32)


def _mod_kernel(c_ref, w_ref, b_ref, o_ref):
    c = c_ref[...]
    s = c * _sigmoid(c)
    o_ref[0] = _dot(s.astype(BF16), w_ref[0].astype(BF16)) + b_ref[0]


def _mod_rows(cvec, w_mod, b_mod):
    depth, d, n = w_mod.shape
    tn = 1024
    rows = cvec.shape[0]
    return pl.pallas_call(
        _mod_kernel,
        grid=(depth, n // tn),
        in_specs=[
            pl.BlockSpec((rows, d), lambda i, j: (0, 0)),
            pl.BlockSpec((1, d, tn), lambda i, j: (i, 0, j)),
            pl.BlockSpec((1, 1, tn), lambda i, j: (i, 0, j)),
        ],
        out_specs=pl.BlockSpec((1, rows, tn), lambda i, j: (i, 0, j)),
        out_shape=jax.ShapeDtypeStruct((depth, rows, n), F32),
        compiler_params=_cparams("arbitrary", "arbitrary"),
        name="mod_rows",
    )(cvec, w_mod, b_mod.reshape(depth, 1, n))


def _inproj_kernel(x_ref, mod_ref, g_ref, w_ref, qg_ref, kg_ref, cos_ref, sin_ref,
                   q_ref, k_ref, v_ref, pin_ref):
    x = x_ref[0]
    h = _rms(x, g_ref[0:1, :]) * (1.0 + mod_ref[0, 1:2, :]) + mod_ref[0, 0:1, :]
    u = _dot(h.astype(BF16), w_ref[...])
    cos = cos_ref[...]
    sin = sin_ref[...]
    lane = lax.broadcasted_iota(jnp.int32, (TM, HEAD_DIM), 1)
    first_half = (lane & 32) == 0

    def head(xh, gain, scale):
        y = _rms(xh, gain)
        sw = jnp.where(first_half, pltpu.roll(y, 96, 1), pltpu.roll(y, 32, 1))
        return ((y * cos + sw * sin) * scale).astype(BF16)

    attn_w = N_HEADS * HEAD_DIM
    kv_w = N_KV_HEADS * HEAD_DIM
    for hh in range(N_HEADS):
        sl = slice(hh * HEAD_DIM, (hh + 1) * HEAD_DIM)
        q_ref[0, :, sl] = head(u[:, sl], qg_ref[...], HEAD_DIM ** -0.5)
    for hh in range(N_KV_HEADS):
        sl = slice(hh * HEAD_DIM, (hh + 1) * HEAD_DIM)
        k_ref[0, :, sl] = head(u[:, attn_w + hh * HEAD_DIM:attn_w + (hh + 1) * HEAD_DIM], kg_ref[...], 1.0)
    v_ref[0] = u[:, attn_w + kv_w:attn_w + 2 * kv_w].astype(BF16)
    pin_ref[0] = u[:, attn_w + 2 * kv_w:]


def _inproj(x_all, mods, g4, w_in_bf, q_gain, k_gain, cos_t, sin_t, n_lat_tiles):
    bn, tt, d = x_all.shape
    nt = tt // TM
    attn_w = N_HEADS * HEAD_DIM
    kv_w = N_KV_HEADS * HEAD_DIM
    pool_w = w_in_bf.shape[1] - attn_w - 2 * kv_w
    mod_map = lambda b, t: (jnp.where(t < n_lat_tiles, b, bn), 0, 0)
    return pl.pallas_call(
        _inproj_kernel,
        grid=(bn, nt),
        in_specs=[
            pl.BlockSpec((1, TM, d), lambda b, t: (b, t, 0)),
            pl.BlockSpec((1, 6, d), mod_map),
            pl.BlockSpec((4, d), lambda b, t: (0, 0)),
            pl.BlockSpec(w_in_bf.shape, lambda b, t: (0, 0)),
            pl.BlockSpec((1, HEAD_DIM), lambda b, t: (0, 0)),
            pl.BlockSpec((1, HEAD_DIM), lambda b, t: (0, 0)),
            pl.BlockSpec((TM, HEAD_DIM), lambda b, t: (t, 0)),
            pl.BlockSpec((TM, HEAD_DIM), lambda b, t: (t, 0)),
        ],
        out_specs=[
            pl.BlockSpec((1, TM, attn_w), lambda b, t: (b, t, 0)),
            pl.BlockSpec((1, TM, kv_w), lambda b, t: (b, t, 0)),
            pl.BlockSpec((1, TM, kv_w), lambda b, t: (b, t, 0)),
            pl.BlockSpec((1, TM, pool_w), lambda b, t: (b, t, 0)),
        ],
        out_shape=[
            jax.ShapeDtypeStruct((bn, tt, attn_w), BF16),
            jax.ShapeDtypeStruct((bn, tt, kv_w), BF16),
            jax.ShapeDtypeStruct((bn, tt, kv_w), BF16),
            jax.ShapeDtypeStruct((bn, tt, pool_w), F32),
        ],
        compiler_params=_cparams("arbitrary", "arbitrary"),
        name="inproj",
    )(x_all, mods, g4, w_in_bf, q_gain.reshape(1, -1), k_gain.reshape(1, -1), cos_t, sin_t)


def _attn_kernel(q_ref, k_ref, v_ref, o_ref, *, n_lat_tiles, lat_len):
    t = pl.program_id(2)

    def attend(kk, vv):
        for i in range(GQA_GROUP):
            sl = slice(i * HEAD_DIM, (i + 1) * HEAD_DIM)
            s = lax.dot_general(q_ref[0, :, sl], kk, (((1,), (1,)), ((), ())), preferred_element_type=F32)
            p = jnp.exp(s - jnp.max(s, axis=-1, keepdims=True))
            l = jnp.sum(p, axis=-1, keepdims=True)
            o_ref[0, :, sl] = (_dot(p.astype(BF16), vv) / l).astype(BF16)

    @pl.when(t < n_lat_tiles)
    def _():
        attend(k_ref[0], v_ref[0])

    @pl.when(t >= n_lat_tiles)
    def _():
        attend(k_ref[0, lat_len:, :], v_ref[0, lat_len:, :])


def _attention(q, k, v, n_lat_tiles):
    bn, tt, attn_w = q.shape
    nt = tt // TM
    gw = GQA_GROUP * HEAD_DIM
    kern = functools.partial(_attn_kernel, n_lat_tiles=n_lat_tiles, lat_len=n_lat_tiles * TM)
    return pl.pallas_call(
        kern,
        grid=(bn, N_KV_HEADS, nt),
        in_specs=[
            pl.BlockSpec((1, TM, gw), lambda b, g, t: (b, t, g)),
            pl.BlockSpec((1, tt, HEAD_DIM), lambda b, g, t: (b, 0, g)),
            pl.BlockSpec((1, tt, HEAD_DIM), lambda b, g, t: (b, 0, g)),
        ],
        out_specs=pl.BlockSpec((1, TM, gw), lambda b, g, t: (b, t, g)),
        out_shape=jax.ShapeDtypeStruct((bn, tt, attn_w), BF16),
        compiler_params=_cparams("arbitrary", "arbitrary", "arbitrary"),
        name="attention",
    )(q, k, v)


def _post_mixer(y, x, mod_ref, g_ref):
    xn = x + mod_ref[0, 2:3, :] * _rms(y, g_ref[1:2, :])
    f = _rms(xn, g_ref[2:3, :]) * (1.0 + mod_ref[0, 4:5, :]) + mod_ref[0, 3:4, :]
    return xn, f


def _mix0_kernel(o_ref, pin_ref, pprev_ref, pnext_ref, pw_ref, ps_ref, wout_ref, x_ref, mod_ref, g_ref,
                 xo_ref, f_ref, ext_sc, cat_sc, *, n_lat_tiles, lat_len, ctx_len):
    t = pl.program_id(1)
    is_ctx = t >= n_lat_tiles
    seq_first = jnp.logical_or(t == 0, is_ctx)
    seq_last = jnp.logical_or(t == n_lat_tiles - 1, is_ctx)
    pin = pin_ref[0]
    ext_sc[0:HALO, :] = jnp.where(seq_first, 0.0, pprev_ref[0])
    ext_sc[HALO:HALO + TM, :] = pin
    ext_sc[HALO + TM:, :] = jnp.where(seq_last, 0.0, pnext_ref[0])
    row = lax.broadcasted_iota(jnp.int32, (TM, 1), 0)
    pos = jnp.where(is_ctx, 0, t * TM) + row
    n = jnp.where(is_ctx, ctx_len, lat_len)
    gw = pin.shape[1] // len(POOL_WINDOWS)
    attn_w = o_ref.shape[2]
    cat_sc[:, 0:attn_w] = o_ref[0]
    for gi, w in enumerate(POOL_WINDOWS):
        sl = slice(gi * gw, (gi + 1) * gw)
        base = HALO - w // 2
        acc = ext_sc[base:base + TM, sl]
        for j in range(1, w):
            acc = acc + ext_sc[base + j:base + j + TM, sl]
        cnt = (jnp.minimum(pos + w // 2, n) - jnp.maximum(pos - w // 2, 0)).astype(F32)
        m = acc / cnt - pin[:, sl]
        y = _dot(m.astype(BF16), pw_ref[gi]) * ps_ref[:, sl]
        cat_sc[:, attn_w + gi * gw:attn_w + (gi + 1) * gw] = y.astype(BF16)
    y = _dot(cat_sc[...], wout_ref[...])
    xn, f = _post_mixer(y, x_ref[0], mod_ref, g_ref)
    xo_ref[0] = xn
    f_ref[0] = f


def _mix0_out(o, pin, pool_w_bf, pool_scale, w_out_bf, x_all, mods, g4, n_lat_tiles, lat_len, ctx_len):
    bn, tt, d = x_all.shape
    nt = tt // TM
    attn_w = o.shape[2]
    pw = pin.shape[2]
    hb = TM // HALO
    n_hblk = tt // HALO
    mod_map = lambda b, t: (jnp.where(t < n_lat_tiles, b, bn), 0, 0)
    kern = functools.partial(_mix0_kernel, n_lat_tiles=n_lat_tiles, lat_len=lat_len, ctx_len=ctx_len)
    return pl.pallas_call(
        kern,
        grid=(bn, nt),
        in_specs=[
            pl.BlockSpec((1, TM, attn_w), lambda b, t: (b, t, 0)),
            pl.BlockSpec((1, TM, pw), lambda b, t: (b, t, 0)),
            pl.BlockSpec((1, HALO, pw), lambda b, t: (b, jnp.maximum(t * hb - 1, 0), 0)),
            pl.BlockSpec((1, HALO, pw), lambda b, t: (b, jnp.minimum((t + 1) * hb, n_hblk - 1), 0)),
            pl.BlockSpec(pool_w_bf.shape, lambda b, t: (0, 0, 0)),
            pl.BlockSpec((1, pw), lambda b, t: (0, 0)),
            pl.BlockSpec(w_out_bf.shape, lambda b, t: (0, 0)),
            pl.BlockSpec((1, TM, d), lambda b, t: (b, t, 0)),
            pl.BlockSpec((1, 6, d), mod_map),
            pl.BlockSpec((4, d), lambda b, t: (0, 0)),
        ],
        out_specs=[
            pl.BlockSpec((1, TM, d), lambda b, t: (b, t, 0)),
            pl.BlockSpec((1, TM, d), lambda b, t: (b, t, 0)),
        ],
        out_shape=[jax.ShapeDtypeStruct((bn, tt, d), F32), jax.ShapeDtypeStruct((bn, tt, d), F32)],
        scratch_shapes=[pltpu.VMEM((TM + 2 * HALO, pw), F32), pltpu.VMEM((TM, attn_w + pw), BF16)],
        compiler_params=_cparams("arbitrary", "arbitrary"),
        name="pool_outproj",
    )(o, pin, pin, pin, pool_w_bf, pool_scale.reshape(1, -1), w_out_bf, x_all, mods, g4)


def _router_kernel(f_ref, rw_ref, rb_ref, idx_ref, gate_ref, rank_ref, cnt_ref, carry_sc):
    i = pl.program_id(0)

    @pl.when(i == 0)
    def _():
        carry_sc[...] = jnp.zeros(carry_sc.shape, F32)

    lane = lax.broadcasted_iota(jnp.int32, (TM, LANES), 1).astype(F32)
    logits = _dot(f_ref[...].astype(BF16), rw_ref[...]) + rb_ref[...]
    l = jnp.where(lane < N_EXPERTS, logits, -jnp.inf)
    vals, sels, idxs = [], [], []
    for _ in range(TOP_K):
        m = jnp.max(l, axis=-1, keepdims=True)
        idx = jnp.min(jnp.where(l == m, lane, float(LANES)), axis=-1, keepdims=True)
        sel = lane == idx
        vals.append(m)
        idxs.append(idx)
        sels.append(sel)
        l = jnp.where(sel, -jnp.inf, l)
    es = [jnp.exp(v - vals[0]) for v in vals]
    denom = es[0] + es[1] + es[2] + es[3]
    onehot = jnp.zeros((TM, LANES), F32)
    for sel in sels:
        onehot = onehot + jnp.where(sel, 1.0, 0.0)
    r_i = lax.broadcasted_iota(jnp.int32, (TM, TM), 0)
    c_i = lax.broadcasted_iota(jnp.int32, (TM, TM), 1)
    tril = jnp.where(c_i < r_i, 1.0, 0.0).astype(BF16)
    before = _dot(tril, onehot.astype(BF16)) + carry_sc[0:1, :]
    idx_out = jnp.zeros((TM, LANES), F32)
    gate_out = jnp.zeros((TM, LANES), F32)
    rank_out = jnp.zeros((TM, LANES), F32)
    for k in range(TOP_K):
        rk = jnp.sum(jnp.where(sels[k], before, 0.0), axis=-1, keepdims=True)
        idx_out = jnp.where(lane == k, idxs[k], idx_out)
        gate_out = jnp.where(lane == k, es[k] / denom, gate_out)
        rank_out = jnp.where(lane == k, rk, rank_out)
    idx_ref[...] = idx_out.astype(jnp.int32)
    gate_ref[...] = gate_out
    rank_ref[...] = rank_out.astype(jnp.int32)
    carry_sc[...] = carry_sc[...] + jnp.sum(onehot, axis=0, keepdims=True)
    cnt_ref[...] = carry_sc[...].astype(jnp.int32)


def _router(f2, router_w, router_b):
    n, d = f2.shape
    rw = jnp.zeros((d, LANES), BF16).at[:, :N_EXPERTS].set(router_w.astype(BF16))
    rb = jnp.zeros((1, LANES), F32).at[0, :N_EXPERTS].set(router_b)
    tile = pl.BlockSpec((TM, LANES), lambda i: (i, 0))
    return pl.pallas_call(
        _router_kernel,
        grid=(n // TM,),
        in_specs=[
            pl.BlockSpec((TM, d), lambda i: (i, 0)),
            pl.BlockSpec((d, LANES), lambda i: (0, 0)),
            pl.BlockSpec((1, LANES), lambda i: (0, 0)),
        ],
        out_specs=[tile, tile, tile, pl.BlockSpec((HALO, LANES), lambda i: (0, 0))],
        out_shape=[
            jax.ShapeDtypeStruct((n, LANES), jnp.int32),
            jax.ShapeDtypeStruct((n, LANES), F32),
            jax.ShapeDtypeStruct((n, LANES), jnp.int32),
            jax.ShapeDtypeStruct((HALO, LANES), jnp.int32),
        ],
        scratch_shapes=[pltpu.VMEM((HALO, LANES), F32)],
        compiler_params=_cparams("arbitrary"),
        name="router",
    )(f2, rw, rb)


def _expert_kernel(be_ref, tok_ref, nb_ref, f_hbm, wg_ref, bg_ref, wu_ref, bu_ref, wd_ref, bd_ref,
                   y_ref, xbuf, sem):
    s = pl.program_id(0)
    nb = nb_ref[0]

    def issue(blk, slot):
        base = blk * TE

        def body(r, carry):
            tok = tok_ref[base + r]
            pltpu.make_async_copy(f_hbm.at[pl.ds(tok, 1)], xbuf.at[slot, pl.ds(r, 1)], sem.at[slot]).start()
            return carry

        lax.fori_loop(0, TE, body, 0, unroll=DMA_UNROLL)

    @pl.when(s == 0)
    def _():
        issue(0, 0)

    @pl.when(s + 1 < nb)
    def _():
        issue(s + 1, (s + 1) % 2)

    @pl.when(s < nb)
    def _():
        slot = s % 2
        pltpu.make_async_copy(f_hbm.at[pl.ds(0, TE)], xbuf.at[slot], sem.at[slot]).wait()
        x = xbuf[slot].astype(BF16)
        glu = jnp.minimum(_dot(x, wg_ref[0]) + bg_ref[0], SWIGLU_LIMIT)
        lin = jnp.clip(_dot(x, wu_ref[0]) + bu_ref[0], -SWIGLU_LIMIT, SWIGLU_LIMIT)
        act = glu * _sigmoid(SWIGLU_ALPHA * glu) * (lin + 1.0)
        y_ref[...] = _dot(act.astype(BF16), wd_ref[0]) + bd_ref[0]

    @pl.when(s >= nb)
    def _():
        y_ref[...] = jnp.zeros(y_ref.shape, F32)


def _experts(f2, block_e, slot_tok, n_used, wg, bg, wu, bu, wd, bd, n_blocks):
    n, d = f2.shape
    ne, _, de = wg.shape
    grid_spec = pltpu.PrefetchScalarGridSpec(
        num_scalar_prefetch=3,
        grid=(n_blocks,),
        in_specs=[
            pl.BlockSpec(memory_space=pl.ANY),
            pl.BlockSpec((1, d, de), lambda s, be, tok, nb: (be[s], 0, 0)),
            pl.BlockSpec((1, 1, de), lambda s, be, tok, nb: (be[s], 0, 0)),
            pl.BlockSpec((1, d, de), lambda s, be, tok, nb: (be[s], 0, 0)),
            pl.BlockSpec((1, 1, de), lambda s, be, tok, nb: (be[s], 0, 0)),
            pl.BlockSpec((1, de, d), lambda s, be, tok, nb: (be[s], 0, 0)),
            pl.BlockSpec((1, 1, d), lambda s, be, tok, nb: (be[s], 0, 0)),
        ],
        out_specs=pl.BlockSpec((TE, d), lambda s, be, tok, nb: (s, 0)),
        scratch_shapes=[pltpu.VMEM((2, TE, d), F32), pltpu.SemaphoreType.DMA((2,))],
    )
    return pl.pallas_call(
        _expert_kernel,
        grid_spec=grid_spec,
        out_shape=jax.ShapeDtypeStruct((n_blocks * TE, d), F32),
        compiler_params=_cparams("arbitrary"),
        name="experts",
    )(block_e, slot_tok, n_used, f2, wg, bg.reshape(ne, 1, de), wu, bu.reshape(ne, 1, de), wd, bd.reshape(ne, 1, d))


def _combine_kernel(dest_ref, tmod_ref, yb_hbm, gate_ref, x_ref, mod_ref, g_ref, xo_ref, buf, sem):
    i = pl.program_id(0)
    nt = pl.num_programs(0)

    def issue(tile, slot):
        base = tile * TM * TOP_K

        def body(r, carry):
            for k in range(TOP_K):
                dst = dest_ref[base + r * TOP_K + k]
                pltpu.make_async_copy(yb_hbm.at[pl.ds(dst, 1)], buf.at[slot, k, pl.ds(r, 1)], sem.at[slot]).start()
            return carry

        lax.fori_loop(0, TM, body, 0, unroll=DMA_UNROLL // 2)

    @pl.when(i == 0)
    def _():
        issue(0, 0)

    @pl.when(i + 1 < nt)
    def _():
        issue(i + 1, (i + 1) % 2)

    slot = i % 2
    for k in range(TOP_K):
        pltpu.make_async_copy(yb_hbm.at[pl.ds(0, TM)], buf.at[slot, k], sem.at[slot]).wait()
    gate = gate_ref[...]
    out = gate[:, 0:1] * buf[slot, 0]
    for k in range(1, TOP_K):
        out = out + gate[:, k:k + 1] * buf[slot, k]
    xo_ref[...] = x_ref[...] + mod_ref[0, 5:6, :] * _rms(out, g_ref[3:4, :])


def _combine(yb, dest, tile_mod, gates, x2, mods, g4):
    n, d = x2.shape
    grid_spec = pltpu.PrefetchScalarGridSpec(
        num_scalar_prefetch=2,
        grid=(n // TM,),
        in_specs=[
            pl.BlockSpec(memory_space=pl.ANY),
            pl.BlockSpec((TM, LANES), lambda i, dst, tm: (i, 0)),
            pl.BlockSpec((TM, d), lambda i, dst, tm: (i, 0)),
            pl.BlockSpec((1, 6, d), lambda i, dst, tm: (tm[i], 0, 0)),
            pl.BlockSpec((4, d), lambda i, dst, tm: (0, 0)),
        ],
        out_specs=pl.BlockSpec((TM, d), lambda i, dst, tm: (i, 0)),
        scratch_shapes=[pltpu.VMEM((2, TOP_K, TM, d), F32), pltpu.SemaphoreType.DMA((2,))],
    )
    return pl.pallas_call(
        _combine_kernel,
        grid_spec=grid_spec,
        out_shape=jax.ShapeDtypeStruct((n, d), F32),
        compiler_params=_cparams("arbitrary"),
        name="moe_combine",
    )(dest, tile_mod, yb, gates, x2, mods, g4)


def _moe(f2, x2, tile_mod, mods, g4, router_w, router_b, wg, bg, wu, bu, wd, bd):
    n, d = f2.shape
    idx, gates, rank, cnt = _router(f2, router_w, router_b)
    idx = idx[:, :TOP_K]
    rank = rank[:, :TOP_K]
    counts = cnt[0, :N_EXPERTS]
    padded = (counts + TE - 1) // TE * TE
    pend = jnp.cumsum(padded)
    pstart = pend - padded
    dest = (pstart[idx] + rank).reshape(-1).astype(jnp.int32)
    n_asg = n * TOP_K
    n_blocks = -(-(n_asg + N_EXPERTS * (TE - 1)) // TE)
    n_used = (pend[-1] // TE).astype(jnp.int32)
    blk = jnp.arange(n_blocks, dtype=jnp.int32)
    blk_start = jnp.minimum(blk, n_used - 1) * TE
    block_e = jnp.sum((blk_start[:, None] >= pend[None, :]).astype(jnp.int32), axis=1)
    block_e = jnp.minimum(block_e, N_EXPERTS - 1)
    slot_tok = jnp.zeros((n_blocks * TE,), jnp.int32).at[dest].set(jnp.repeat(jnp.arange(n, dtype=jnp.int32), TOP_K))
    yb = _experts(f2, block_e, slot_tok, n_used.reshape(1), wg.astype(BF16), bg, wu.astype(BF16), bu,
                  wd.astype(BF16), bd, n_blocks)
    return _combine(yb, dest, tile_mod, gates, x2, mods, g4)


def _normmod_kernel(x_ref, mod_ref, g_ref, h_ref):
    h_ref[0] = _rms(x_ref[0], g_ref[0:1, :]) * (1.0 + mod_ref[0, 1:2, :]) + mod_ref[0, 0:1, :]


def _normmod(x_all, mods, g4, n_lat_tiles):
    bn, tt, d = x_all.shape
    mod_map = lambda b, t: (jnp.where(t < n_lat_tiles, b, bn), 0, 0)
    tile = pl.BlockSpec((1, TM, d), lambda b, t: (b, t, 0))
    return pl.pallas_call(
        _normmod_kernel,
        grid=(bn, tt // TM),
        in_specs=[tile, pl.BlockSpec((1, 6, d), mod_map), pl.BlockSpec((4, d), lambda b, t: (0, 0))],
        out_specs=tile,
        out_shape=jax.ShapeDtypeStruct((bn, tt, d), F32),
        compiler_params=_cparams("arbitrary", "arbitrary"),
        name="norm_mod",
    )(x_all, mods, g4)


def _centred_shift(h, hprev_ref, hnext_ref, t, n_lat_tiles, n_tiles):
    seq_first = jnp.logical_or(t == 0, t == n_lat_tiles)
    seq_last = jnp.logical_or(t == n_lat_tiles - 1, t == n_tiles - 1)
    prev_row = jnp.where(seq_first, 0.0, hprev_ref[0, HALO - 1:HALO, :])
    next_row = jnp.where(seq_last, 0.0, hnext_ref[0, 0:1, :])
    tm = h.shape[0]
    row = lax.broadcasted_iota(jnp.int32, (tm, 1), 0)
    prev = jnp.where(row == 0, prev_row, pltpu.roll(h, 1, 0))
    nxt = jnp.where(row == tm - 1, next_row, pltpu.roll(h, tm - 1, 0))
    return 0.5 * (prev + nxt) - h


def _halo_specs(d, tt, tm):
    hb = tm // HALO
    n_hblk = tt // HALO
    return [
        pl.BlockSpec((1, tm, d), lambda *a: (a[-2], a[-1], 0)),
        pl.BlockSpec((1, HALO, d), lambda *a: (a[-2], jnp.maximum(a[-1] * hb - 1, 0), 0)),
        pl.BlockSpec((1, HALO, d), lambda *a: (a[-2], jnp.minimum((a[-1] + 1) * hb, n_hblk - 1), 0)),
    ]


def _shiftproj_kernel(h_ref, hprev_ref, hnext_ref, mix_ref, w_ref, o_ref, *, n_lat_tiles):
    t = pl.program_id(2)
    h = h_ref[0]
    xx = _centred_shift(h, hprev_ref, hnext_ref, t, n_lat_tiles, pl.num_programs(2))
    o_ref[0, 0] =_dot((h + xx * mix_ref[0]).astype(BF16), w_ref[0])


def _shiftproj(h, mix3, w3_bf, n_lat_tiles):
    bn, tt, d = h.shape
    nj = w3_bf.shape[0]
    kern = functools.partial(_shiftproj_kernel, n_lat_tiles=n_lat_tiles)
    return pl.pallas_call(
        kern,
        grid=(nj, bn, tt // TM),
        in_specs=_halo_specs(d, tt, TM) + [
            pl.BlockSpec((1, 1, d), lambda j, b, t: (j, 0, 0)),
            pl.BlockSpec((1, d, d), lambda j, b, t: (j, 0, 0)),
        ],
        out_specs=pl.BlockSpec((1, 1, TM, d), lambda j, b, t: (j, b, t, 0)),
        out_shape=jax.ShapeDtypeStruct((nj, bn, tt, d), F32),
        compiler_params=_cparams("arbitrary", "arbitrary", "arbitrary"),
        name="shift_proj",
    )(h, h, h, mix3, w3_bf)


def _softplus(x):
    return jnp.maximum(x, 0.0) + jnp.log(1.0 + jnp.exp(-jnp.abs(x)))


def _lora_kernel(h_ref, hprev_ref, hnext_ref, mix_ref, w1_ref, a1_ref, g1_ref, w2_ref, a2_ref, g2_ref,
                 w0_ref, a0_ref, dec_ref, aa_ref, gate_ref, *, n_lat_tiles):
    t = pl.program_id(1)
    h = h_ref[0]
    xx = _centred_shift(h, hprev_ref, hnext_ref, t, n_lat_tiles, pl.num_programs(1))
    hw = jnp.tanh(_dot((h + xx * mix_ref[0:1, :]).astype(BF16), w1_ref[...]))
    ha = _dot((h + xx * mix_ref[1:2, :]).astype(BF16), a1_ref[...])
    hg = _sigmoid(_dot((h + xx * mix_ref[2:3, :]).astype(BF16), g1_ref[...]))
    gate_ref[0] = _dot(hg.astype(BF16), g2_ref[...])
    for dd in range(2):
        sl = slice(dd * LORA_PAD, (dd + 1) * LORA_PAD)
        z = w0_ref[dd] + _dot(hw[:, sl].astype(BF16), w2_ref[dd])
        w_log = -_softplus(-z) - 0.5
        dec_ref[dd, 0] = jnp.exp(-jnp.exp(w_log))
        aa_ref[dd, 0] = _sigmoid(a0_ref[dd] + _dot(ha[:, sl].astype(BF16), a2_ref[dd]))


def _pad_lora(w_in, w_out):
    nd, d, r = w_in.shape
    wi = jnp.zeros((d, nd * LORA_PAD), BF16)
    wo = jnp.zeros((nd, LORA_PAD, d), BF16)
    for dd in range(nd):
        wi = wi.at[:, dd * LORA_PAD:dd * LORA_PAD + r].set(w_in[dd].astype(BF16))
        wo = wo.at[dd, :r].set(w_out[dd].astype(BF16))
    return wi, wo


def _lora(h, mix3, w1, w2, a1, a2, g1, g2, w0, a0, n_lat_tiles):
    bn, tt, d = h.shape
    w1p, w2p = _pad_lora(w1, w2)
    a1p, a2p = _pad_lora(a1, a2)
    const2 = lambda s: pl.BlockSpec(s, lambda b, t: (0, 0))
    const3 = lambda s: pl.BlockSpec(s, lambda b, t: (0, 0, 0))
    kern = functools.partial(_lora_kernel, n_lat_tiles=n_lat_tiles * (TM // TL))
    dir_out = pl.BlockSpec((2, 1, TL, d), lambda b, t: (0, b, t, 0))
    tok_out = pl.BlockSpec((1, TL, d), lambda b, t: (b, t, 0))
    dir_shape = jax.ShapeDtypeStruct((2, bn, tt, d), F32)
    tok_shape = jax.ShapeDtypeStruct((bn, tt, d), F32)
    return pl.pallas_call(
        kern,
        grid=(bn, tt // TL),
        in_specs=_halo_specs(d, tt, TL) + [
            const2((3, d)), const2(w1p.shape), const2(a1p.shape), const2((d, g1.shape[1])),
            const3(w2p.shape), const3(a2p.shape), const2(g2.shape),
            const3((2, 1, d)), const3((2, 1, d)),
        ],
        out_specs=[dir_out, dir_out, tok_out],
        out_shape=[dir_shape, dir_shape, tok_shape],
        compiler_params=_cparams("arbitrary", "arbitrary"),
        name="rwkv_lora",
    )(h, h, h, mix3, w1p, a1p, g1.astype(BF16), w2p, a2p, g2.astype(BF16),
      w0.reshape(2, 1, d), a0.reshape(2, 1, d))


N_CH = RWKV_HEAD
N_RH = 32
N_GRP = LANES // N_RH


def _lane_group(rows):
    return lax.broadcasted_iota(jnp.int32, (rows, LANES), 1) // N_RH


def _pick_groups(pieces, lg):
    out = pieces[N_GRP - 1]
    for g in range(N_GRP - 2, -1, -1):
        out = jnp.where(lg == g, pieces[g], out)
    return out


def _to_chain_blocks(srcs, j, lg):
    blks = [s[:, j * LANES:(j + 1) * LANES] for s in srcs]
    rolled = []
    for sh in range(N_GRP):
        merged = _pick_groups([blks[(i + sh) % N_GRP] for i in range(N_GRP)], lg)
        rolled.append(merged if sh == 0 else pltpu.roll(merged, sh * N_RH, 1))
    return [_pick_groups([rolled[(g - i) % N_GRP] for g in range(N_GRP)], lg) for i in range(N_GRP)]


def _head_sum(x):
    s = x[:, 0:LANES]
    for j in range(1, x.shape[1] // LANES):
        s = s + x[:, j * LANES:(j + 1) * LANES]
    s = s + pltpu.roll(s, 2 * N_RH, 1)
    return s + pltpu.roll(s, N_RH, 1)


def _prep_kernel(rkv_ref, dec_ref, aa_ref, kk_ref, ka_ref, w_ref, kd_ref, bb_ref, nk_ref, v_ref, r_ref):
    rows = rkv_ref.shape[2]
    bn = rkv_ref.shape[1]
    lg = _lane_group(rows)
    nblk = kk_ref.shape[1] // LANES
    ks, kks = [], []
    for b in range(bn):
        k = rkv_ref[1, b]
        kk = k * kk_ref[...]
        inv = lax.rsqrt(jnp.maximum(_head_sum(kk * kk), 1e-24))
        ks.append(k)
        kks.append(kk * jnp.concatenate([inv] * nblk, axis=1))
    groups = [(dd, b) for dd in range(2) for b in range(bn)]
    kd_src = [ks[b] * (1.0 + (aa_ref[dd, b] - 1.0) * ka_ref[...]) for dd, b in groups]
    bb_src = [kks[b] * aa_ref[dd, b] for dd, b in groups]
    w_src = [dec_ref[dd, b] for dd, b in groups]
    nk_src = [-kks[b] for _, b in groups]
    v_src = [rkv_ref[2, b] for _, b in groups]
    r_src = [rkv_ref[0, b] for _, b in groups]
    for j in range(N_CH // N_GRP):
        for srcs, ref in ((w_src, w_ref), (kd_src, kd_ref), (bb_src, bb_ref), (nk_src, nk_ref),
                          (v_src, v_ref), (r_src, r_ref)):
            for i, blk in enumerate(_to_chain_blocks(srcs, j, lg)):
                c = j * N_GRP + i
                ref[:, c * LANES:(c + 1) * LANES] = blk


def _prep(rkv, dec, aa, k_k, k_a):
    _, bn, tt, d = rkv.shape
    assert bn * 2 == N_GRP and d == N_CH * N_RH
    out = pl.BlockSpec((TP, N_CH * LANES), lambda t: (t, 0))
    shape = jax.ShapeDtypeStruct((tt, N_CH * LANES), F32)
    return pl.pallas_call(
        _prep_kernel,
        grid=(tt // TP,),
        in_specs=[
            pl.BlockSpec((3, bn, TP, d), lambda t: (0, 0, t, 0)),
            pl.BlockSpec((2, bn, TP, d), lambda t: (0, 0, t, 0)),
            pl.BlockSpec((2, bn, TP, d), lambda t: (0, 0, t, 0)),
            pl.BlockSpec((1, d), lambda t: (0, 0)),
            pl.BlockSpec((1, d), lambda t: (0, 0)),
        ],
        out_specs=[out] * 6,
        out_shape=[shape] * 6,
        compiler_params=_cparams("arbitrary"),
        name="rwkv_prep",
    )(rkv, dec, aa, k_k.reshape(1, d), k_a.reshape(1, d))


def _scan_kernel(*refs, steps):
    fwd_refs, rev_refs = refs[0:6], refs[6:12]
    yf_ref, yr_ref, s_sc, op_sc, sa_sc = refs[12:17]
    W, K, B, A, V, R, WR = range(7)
    n_oct = N_CH // HALO
    assert steps % 2 == 0

    @pl.when(pl.program_id(0) == 0)
    def _():
        s_sc[...] = jnp.zeros(s_sc.shape, F32)
        op_sc[...] = jnp.zeros(op_sc.shape, F32)
        sa_sc[...] = jnp.zeros(sa_sc.shape, F32)

    is_fwd = lax.broadcasted_iota(jnp.int32, (1, LANES), 1) < LANES // 2
    octet = lambda j: slice(j * HALO, (j + 1) * HALO)

    def step(i, carry):
        ir = steps - 1 - i
        cur = i % 2
        prv = 1 - cur
        for n, (f, g) in enumerate(zip(fwd_refs, rev_refs)):
            op_sc[cur, n] = jnp.where(is_fwd, f[i], g[ir])
        r = op_sc[cur, R]
        op_sc[cur, WR] = op_sc[cur, W] * r
        br = jnp.sum(op_sc[cur, B] * r, axis=0, keepdims=True)
        kr = jnp.sum(op_sc[cur, K] * r, axis=0, keepdims=True)
        sa_p = [sa_sc[octet(j), :] for j in range(n_oct)]
        vv_p = [op_sc[prv, V, octet(j), :] for j in range(n_oct)]

        def sweep(g, acc):
            sa, y0 = list(acc[:n_oct]), list(acc[n_oct:])
            for u in range(SCAN_UNROLL):
                kk = g * SCAN_UNROLL + u
                w_row = op_sc[prv, W, pl.ds(kk, 1), :]
                b_row = op_sc[prv, B, pl.ds(kk, 1), :]
                k_row = op_sc[prv, K, pl.ds(kk, 1), :]
                a_row = op_sc[cur, A, pl.ds(kk, 1), :]
                wr_row = op_sc[cur, WR, pl.ds(kk, 1), :]
                for j in range(n_oct):
                    s = s_sc[kk, octet(j), :] * w_row + sa_p[j] * b_row + vv_p[j] * k_row
                    s_sc[kk, octet(j), :] = s
                    sa[j] = sa[j] + s * a_row
                    y0[j] = y0[j] + s * wr_row
            return tuple(sa) + tuple(y0)

        zero = jnp.zeros((HALO, LANES), F32)
        acc = lax.fori_loop(0, N_CH // SCAN_UNROLL, sweep, (zero,) * (2 * n_oct))
        for j in range(n_oct):
            sa_j, y0_j = acc[j], acc[n_oct + j]
            sa_sc[octet(j), :] = sa_j
            y = y0_j + sa_j * br + op_sc[cur, V, octet(j), :] * kr
            yf_ref[i, octet(j), :] = y
            yr_ref[ir, octet(j), :] = y
        return carry

    lax.fori_loop(0, steps, step, 0)


def _scan(ops, lat_len, ctx_len, steps=32):
    tt = ops[0].shape[0]
    n_lat, n_ctx = lat_len // steps, ctx_len // steps
    ops = [x.reshape(tt, N_CH, LANES) for x in ops]

    def fwd_blk(i):
        return jnp.where(i < n_ctx, n_lat + i, i - n_ctx)

    def rev_blk(i):
        return jnp.where(i < n_ctx, n_lat + n_ctx - 1 - i, n_lat - 1 - (i - n_ctx))

    fwd = pl.BlockSpec((steps, N_CH, LANES), lambda i: (fwd_blk(i), 0, 0))
    rev = pl.BlockSpec((steps, N_CH, LANES), lambda i: (rev_blk(i), 0, 0))
    shape = jax.ShapeDtypeStruct((tt, N_CH, LANES), F32)
    return pl.pallas_call(
        functools.partial(_scan_kernel, steps=steps),
        grid=(tt // steps,),
        in_specs=[fwd] * 6 + [rev] * 6,
        out_specs=[fwd, rev],
        out_shape=[shape, shape],
        scratch_shapes=[pltpu.VMEM((N_CH, N_CH, LANES), F32), pltpu.VMEM((2, 7, N_CH, LANES), F32),
                        pltpu.VMEM((N_CH, LANES), F32)],
        compiler_params=_cparams("arbitrary"),
        name="rwkv_scan",
    )(*ops, *ops)


def _readout_kernel(yf_ref, yr_ref, r_ref, kd_ref, v_ref, gate_ref, lnw_ref, lnb_ref, rk_ref, wo_ref,
                    x_ref, mod_ref, g_ref, xo_ref, f_ref, o_sc):
    rows = yf_ref.shape[0]
    bn = x_ref.shape[0]
    half = LANES // 2
    lg = _lane_group(rows)
    blk = lambda ref, c: ref[:, c * LANES:(c + 1) * LANES]
    ys = [blk(yf_ref, c) + pltpu.roll(blk(yr_ref, c), half, 1) for c in range(N_CH)]
    inv_n = 1.0 / N_CH
    mu = ys[0]
    for c in range(1, N_CH):
        mu = mu + ys[c]
    mu = mu * inv_n
    var = (ys[0] - mu) * (ys[0] - mu)
    for c in range(1, N_CH):
        var = var + (ys[c] - mu) * (ys[c] - mu)
    rstd = lax.rsqrt(var * inv_n + GN_EPS)
    s = blk(r_ref, 0) * blk(rk_ref, 0) * blk(kd_ref, 0)
    for c in range(1, N_CH):
        s = s + blk(r_ref, c) * blk(rk_ref, c) * blk(kd_ref, c)
    s = s + pltpu.roll(s, half, 1)
    for c in range(N_CH):
        o_sc[:, c * LANES:(c + 1) * LANES] = ((ys[c] - mu) * rstd * blk(lnw_ref, c) + blk(lnb_ref, c)
                                              + s * blk(v_ref, c))
    for b in range(bn):
        cols = []
        for j in range(N_CH // N_GRP):
            pieces = []
            for i in range(N_GRP):
                c = j * N_GRP + i
                shift = ((i - b) % N_GRP) * N_RH
                piece = o_sc[:, c * LANES:(c + 1) * LANES]
                pieces.append(piece if shift == 0 else pltpu.roll(piece, shift, 1))
            cols.append(_pick_groups(pieces, lg))
        o = jnp.concatenate(cols, axis=1) * gate_ref[b]
        yo = _dot(o.astype(BF16), wo_ref[...])
        xn = x_ref[b] + mod_ref[b, 2:3, :] * _rms(yo, g_ref[1:2, :])
        xo_ref[b] = xn
        f_ref[b] = _rms(xn, g_ref[2:3, :]) * (1.0 + mod_ref[b, 4:5, :]) + mod_ref[b, 3:4, :]


def _chain_param(p):
    pc = p.reshape(N_RH, N_CH).T
    return jnp.tile(pc, (1, N_GRP)).reshape(1, N_CH * LANES)


def _readout(yf, yr, r_c, kd_c, v_c, gate, ln_w, ln_b, r_k, wo_bf, x_all, mods, g4, lat_len):
    bn, tt, d = x_all.shape
    cw = N_CH * LANES
    chain = pl.BlockSpec((TP, cw), lambda t: (t, 0))
    tok = pl.BlockSpec((bn, TP, d), lambda t: (0, t, 0))
    row = pl.BlockSpec((1, cw), lambda t: (0, 0))
    out_shape = jax.ShapeDtypeStruct((bn, lat_len, d), F32)
    return pl.pallas_call(
        _readout_kernel,
        grid=(lat_len // TP,),
        in_specs=[
            chain, chain, chain, chain, chain, tok, row, row, row,
            pl.BlockSpec((d, d), lambda t: (0, 0)),
            tok,
            pl.BlockSpec((bn + 1, 6, d), lambda t: (0, 0, 0)),
            pl.BlockSpec((4, d), lambda t: (0, 0)),
        ],
        out_specs=[tok, tok],
        out_shape=[out_shape, out_shape],
        scratch_shapes=[pltpu.VMEM((TP, cw), F32)],
        compiler_params=_cparams("arbitrary"),
        name="rwkv_readout",
    )(yf.reshape(tt, cw), yr.reshape(tt, cw), r_c, kd_c, v_c, gate, _chain_param(ln_w), _chain_param(ln_b),
      _chain_param(r_k), wo_bf, x_all, mods, g4)


def _rope_tables(lat_len, ctx_len):
    m = HEAD_DIM // 4
    t = jnp.arange(lat_len)
    inv = ROPE_THETA ** (-jnp.arange(0, 2 * m, 2, dtype=F32) / (2 * m))
    ar = (t // GRID_W).astype(F32)[:, None] * inv[None, :]
    ac = (t % GRID_W).astype(F32)[:, None] * inv[None, :]
    cos = jnp.concatenate([jnp.cos(ar), jnp.cos(ar), jnp.cos(ac), jnp.cos(ac)], axis=-1)
    sin = jnp.concatenate([-jnp.sin(ar), jnp.sin(ar), -jnp.sin(ac), jnp.sin(ac)], axis=-1)
    cos = jnp.concatenate([cos, jnp.ones((ctx_len, HEAD_DIM), F32)], axis=0)
    sin = jnp.concatenate([sin, jnp.zeros((ctx_len, HEAD_DIM), F32)], axis=0)
    return cos, sin


def kernel(x, c, ctx, c_ctx, w_mod, b_mod, norm_g, ab_w_in, ab_q_gain, ab_k_gain, pool_w, pool_scale, ab_w_out, rw_mix, rw_wr, rw_wk, rw_wv, rw_wo, rw_w0, rw_w1, rw_w2, rw_a0, rw_a1, rw_a2, rw_g1, rw_g2, rw_k_k, rw_k_a, rw_r_k, rw_ln_w, rw_ln_b, router_w, router_b, moe_w_gate, moe_b_gate, moe_w_up, moe_b_up, moe_w_down, moe_b_down):
    bn, lat_len, d = x.shape
    ctx_len = ctx.shape[1]
    assert lat_len % TM == 0 and ctx_len == TM and d % LANES == 0
    n_lat_tiles = lat_len // TM
    tt = lat_len + ctx_len
    nt = tt // TM

    cvec = jnp.zeros((HALO, d), F32).at[:bn].set(c).at[bn].set(c_ctx)
    mods_all = _mod_rows(cvec, w_mod, b_mod)
    mods = [mods_all[i, :bn + 1].reshape(bn + 1, 6, d) for i in range(w_mod.shape[0])]

    x_all = jnp.concatenate([x, ctx], axis=1)

    cos_t, sin_t = _rope_tables(lat_len, ctx_len)
    q, k, v, pin = _inproj(x_all, mods[0], norm_g[0], ab_w_in[0].astype(BF16), ab_q_gain[0], ab_k_gain[0],
                           cos_t, sin_t, n_lat_tiles)
    o = _attention(q, k, v, n_lat_tiles)
    x_all, f = _mix0_out(o, pin, pool_w[0].astype(BF16), pool_scale[0], ab_w_out[0].astype(BF16), x_all,
                         mods[0], norm_g[0], n_lat_tiles, lat_len, ctx_len)
    tile_mod = jnp.where(jnp.arange(bn * nt) % nt < n_lat_tiles, jnp.arange(bn * nt) // nt, bn).astype(jnp.int32)
    x_all = _moe(f.reshape(bn * tt, d), x_all.reshape(bn * tt, d), tile_mod, mods[0], norm_g[0],
                 router_w[0], router_b[0], moe_w_gate[0], moe_b_gate[0], moe_w_up[0], moe_b_up[0],
                 moe_w_down[0], moe_b_down[0]).reshape(bn, tt, d)

    h = _normmod(x_all, mods[1], norm_g[1], n_lat_tiles)
    mix = rw_mix[0]
    perm = jnp.arange(d).reshape(N_RH, N_CH).T.reshape(-1)
    w3 = jnp.stack([rw_wr[0], rw_wk[0], rw_wv[0]])[:, :, perm].astype(BF16)
    rkv = _shiftproj(h, jnp.stack([mix[0], mix[2], mix[3]])[:, None, :], w3, n_lat_tiles)
    dec, aa, gate = _lora(h, jnp.stack([mix[1], mix[4], mix[5]]), rw_w1[0], rw_w2[0][:, :, perm], rw_a1[0],
                          rw_a2[0][:, :, perm], rw_g1[0], rw_g2[0][:, perm], rw_w0[0][:, perm], rw_a0[0][:, perm],
                          n_lat_tiles)
    w_c, kd_c, bb_c, nk_c, v_c, r_c = _prep(rkv, dec, aa, rw_k_k[0][perm], rw_k_a[0][perm])
    yf, yr = _scan([w_c, kd_c, bb_c, nk_c, v_c, r_c], lat_len, ctx_len)
    x_lat, f = _readout(yf, yr, r_c, kd_c, v_c, gate, rw_ln_w[0], rw_ln_b[0], rw_r_k[0].reshape(-1),
                        rw_wo[0][perm, :].astype(BF16), x_all, mods[1], norm_g[1], lat_len)
    tile_mod = (jnp.arange(bn * n_lat_tiles) // n_lat_tiles).astype(jnp.int32)
    out = _moe(f.reshape(bn * lat_len, d), x_lat.reshape(bn * lat_len, d), tile_mod, mods[1], norm_g[1],
               router_w[1], router_b[1], moe_w_gate[1], moe_b_gate[1], moe_w_up[1], moe_b_up[1],
               moe_w_down[1], moe_b_down[1])
    return out.reshape(bn, lat_len, d)
```

```python
import functools

import jax
import jax.numpy as jnp
from jax import lax
from jax.experimental import pallas as pl
from jax.experimental.pallas import tpu as pltpu

F32 = jnp.float32
BF16 = jnp.bfloat16

TM = 256
TL = 128
TP = 64
TE = 256
SCAN_UNROLL = 16
DMA_UNROLL = 8
LANES = 128
HALO = 8
VMEM_LIMIT = 56 * 1024 * 1024

RMS_EPS = 1e-6
GN_EPS = 64e-5
HEAD_DIM = 128
N_HEADS = 8
N_KV_HEADS = 2
GQA_GROUP = N_HEADS // N_KV_HEADS
GRID_W = 64
ROPE_THETA = 10000.0
POOL_WINDOWS = (2, 4, 8, 16)
RWKV_HEAD = 64
LORA_PAD = 128
N_EXPERTS = 32
TOP_K = 4
SWIGLU_LIMIT = 7.0
SWIGLU_ALPHA = 1.702


def _cparams(*sem):
    return pltpu.CompilerParams(dimension_semantics=sem, vmem_limit_bytes=VMEM_LIMIT)


def _rms(x, g):
    return x * lax.rsqrt(jnp.mean(x * x, axis=-1, keepdims=True) + RMS_EPS) * g


def _sigmoid(x):
    return 1.0 / (1.0 + jnp.exp(-x))


def _dot(a, b):
    return jnp.dot(a, b, preferred_element_type=F32)


def _mod_kernel(c_ref, w_ref, b_ref, o_ref):
    c = c_ref[...]
    s = c * _sigmoid(c)
    o_ref[0] = _dot(s.astype(BF16), w_ref[0].astype(BF16)) + b_ref[0]


def _mod_rows(cvec, w_mod, b_mod):
    depth, d, n = w_mod.shape
    tn = 1024
    rows = cvec.shape[0]
    return pl.pallas_call(
        _mod_kernel,
        grid=(depth, n // tn),
        in_specs=[
            pl.BlockSpec((rows, d), lambda i, j: (0, 0)),
            pl.BlockSpec((1, d, tn), lambda i, j: (i, 0, j)),
            pl.BlockSpec((1, 1, tn), lambda i, j: (i, 0, j)),
        ],
        out_specs=pl.BlockSpec((1, rows, tn), lambda i, j: (i, 0, j)),
        out_shape=jax.ShapeDtypeStruct((depth, rows, n), F32),
        compiler_params=_cparams("arbitrary", "arbitrary"),
        name="mod_rows",
    )(cvec, w_mod, b_mod.reshape(depth, 1, n))


def _inproj_kernel(x_ref, mod_ref, g_ref, w_ref, qg_ref, kg_ref, cos_ref, sin_ref,
                   q_ref, k_ref, v_ref, pin_ref):
    x = x_ref[0]
    h = _rms(x, g_ref[0:1, :]) * (1.0 + mod_ref[0, 1:2, :]) + mod_ref[0, 0:1, :]
    u = _dot(h.astype(BF16), w_ref[...])
    cos = cos_ref[...]
    sin = sin_ref[...]
    lane = lax.broadcasted_iota(jnp.int32, (TM, HEAD_DIM), 1)
    first_half = (lane & 32) == 0

    def head(xh, gain, scale):
        y = _rms(xh, gain)
        sw = jnp.where(first_half, pltpu.roll(y, 96, 1), pltpu.roll(y, 32, 1))
        return ((y * cos + sw * sin) * scale).astype(BF16)

    attn_w = N_HEADS * HEAD_DIM
    kv_w = N_KV_HEADS * HEAD_DIM
    for hh in range(N_HEADS):
        sl = slice(hh * HEAD_DIM, (hh + 1) * HEAD_DIM)
        q_ref[0, :, sl] = head(u[:, sl], qg_ref[...], HEAD_DIM ** -0.5)
    for hh in range(N_KV_HEADS):
        sl = slice(hh * HEAD_DIM, (hh + 1) * HEAD_DIM)
        k_ref[0, :, sl] = head(u[:, attn_w + hh * HEAD_DIM:attn_w + (hh + 1) * HEAD_DIM], kg_ref[...], 1.0)
    v_ref[0] = u[:, attn_w + kv_w:attn_w + 2 * kv_w].astype(BF16)
    pin_ref[0] = u[:, attn_w + 2 * kv_w:]


def _inproj(x_all, mods, g4, w_in_bf, q_gain, k_gain, cos_t, sin_t, n_lat_tiles):
    bn, tt, d = x_all.shape
    nt = tt // TM
    attn_w = N_HEADS * HEAD_DIM
    kv_w = N_KV_HEADS * HEAD_DIM
    pool_w = w_in_bf.shape[1] - attn_w - 2 * kv_w
    mod_map = lambda b, t: (jnp.where(t < n_lat_tiles, b, bn), 0, 0)
    return pl.pallas_call(
        _inproj_kernel,
        grid=(bn, nt),
        in_specs=[
            pl.BlockSpec((1, TM, d), lambda b, t: (b, t, 0)),
            pl.BlockSpec((1, 6, d), mod_map),
            pl.BlockSpec((4, d), lambda b, t: (0, 0)),
            pl.BlockSpec(w_in_bf.shape, lambda b, t: (0, 0)),
            pl.BlockSpec((1, HEAD_DIM), lambda b, t: (0, 0)),
            pl.BlockSpec((1, HEAD_DIM), lambda b, t: (0, 0)),
            pl.BlockSpec((TM, HEAD_DIM), lambda b, t: (t, 0)),
            pl.BlockSpec((TM, HEAD_DIM), lambda b, t: (t, 0)),
        ],
        out_specs=[
            pl.BlockSpec((1, TM, attn_w), lambda b, t: (b, t, 0)),
            pl.BlockSpec((1, TM, kv_w), lambda b, t: (b, t, 0)),
            pl.BlockSpec((1, TM, kv_w), lambda b, t: (b, t, 0)),
            pl.BlockSpec((1, TM, pool_w), lambda b, t: (b, t, 0)),
        ],
        out_shape=[
            jax.ShapeDtypeStruct((bn, tt, attn_w), BF16),
            jax.ShapeDtypeStruct((bn, tt, kv_w), BF16),
            jax.ShapeDtypeStruct((bn, tt, kv_w), BF16),
            jax.ShapeDtypeStruct((bn, tt, pool_w), F32),
        ],
        compiler_params=_cparams("arbitrary", "arbitrary"),
        name="inproj",
    )(x_all, mods, g4, w_in_bf, q_gain.reshape(1, -1), k_gain.reshape(1, -1), cos_t, sin_t)


def _attn_kernel(q_ref, k_ref, v_ref, o_ref, *, n_lat_tiles, lat_len):
    t = pl.program_id(2)

    def attend(kk, vv):
        for i in range(GQA_GROUP):
            sl = slice(i * HEAD_DIM, (i + 1) * HEAD_DIM)
            s = lax.dot_general(q_ref[0, :, sl], kk, (((1,), (1,)), ((), ())), preferred_element_type=F32)
            p = jnp.exp(s - jnp.max(s, axis=-1, keepdims=True))
            l = jnp.sum(p, axis=-1, keepdims=True)
            o_ref[0, :, sl] = (_dot(p.astype(BF16), vv) / l).astype(BF16)

    @pl.when(t < n_lat_tiles)
    def _():
        attend(k_ref[0], v_ref[0])

    @pl.when(t >= n_lat_tiles)
    def _():
        attend(k_ref[0, lat_len:, :], v_ref[0, lat_len:, :])


def _attention(q, k, v, n_lat_tiles):
    bn, tt, attn_w = q.shape
    nt = tt // TM
    gw = GQA_GROUP * HEAD_DIM
    kern = functools.partial(_attn_kernel, n_lat_tiles=n_lat_tiles, lat_len=n_lat_tiles * TM)
    return pl.pallas_call(
        kern,
        grid=(bn, N_KV_HEADS, nt),
        in_specs=[
            pl.BlockSpec((1, TM, gw), lambda b, g, t: (b, t, g)),
            pl.BlockSpec((1, tt, HEAD_DIM), lambda b, g, t: (b, 0, g)),
            pl.BlockSpec((1, tt, HEAD_DIM), lambda b, g, t: (b, 0, g)),
        ],
        out_specs=pl.BlockSpec((1, TM, gw), lambda b, g, t: (b, t, g)),
        out_shape=jax.ShapeDtypeStruct((bn, tt, attn_w), BF16),
        compiler_params=_cparams("arbitrary", "arbitrary", "arbitrary"),
        name="attention",
    )(q, k, v)


def _post_mixer(y, x, mod_ref, g_ref):
    xn = x + mod_ref[0, 2:3, :] * _rms(y, g_ref[1:2, :])
    f = _rms(xn, g_ref[2:3, :]) * (1.0 + mod_ref[0, 4:5, :]) + mod_ref[0, 3:4, :]
    return xn, f


def _mix0_kernel(o_ref, pin_ref, pprev_ref, pnext_ref, pw_ref, ps_ref, wout_ref, x_ref, mod_ref, g_ref,
                 xo_ref, f_ref, ext_sc, cat_sc, *, n_lat_tiles, lat_len, ctx_len):
    t = pl.program_id(1)
    is_ctx = t >= n_lat_tiles
    seq_first = jnp.logical_or(t == 0, is_ctx)
    seq_last = jnp.logical_or(t == n_lat_tiles - 1, is_ctx)
    pin = pin_ref[0]
    ext_sc[0:HALO, :] = jnp.where(seq_first, 0.0, pprev_ref[0])
    ext_sc[HALO:HALO + TM, :] = pin
    ext_sc[HALO + TM:, :] = jnp.where(seq_last, 0.0, pnext_ref[0])
    row = lax.broadcasted_iota(jnp.int32, (TM, 1), 0)
    pos = jnp.where(is_ctx, 0, t * TM) + row
    n = jnp.where(is_ctx, ctx_len, lat_len)
    gw = pin.shape[1] // len(POOL_WINDOWS)
    attn_w = o_ref.shape[2]
    cat_sc[:, 0:attn_w] = o_ref[0]
    for gi, w in enumerate(POOL_WINDOWS):
        sl = slice(gi * gw, (gi + 1) * gw)
        base = HALO - w // 2
        acc = ext_sc[base:base + TM, sl]
        for j in range(1, w):
            acc = acc + ext_sc[base + j:base + j + TM, sl]
        cnt = (jnp.minimum(pos + w // 2, n) - jnp.maximum(pos - w // 2, 0)).astype(F32)
        m = acc / cnt - pin[:, sl]
        y = _dot(m.astype(BF16), pw_ref[gi]) * ps_ref[:, sl]
        cat_sc[:, attn_w + gi * gw:attn_w + (gi + 1) * gw] = y.astype(BF16)
    y = _dot(cat_sc[...], wout_ref[...])
    xn, f = _post_mixer(y, x_ref[0], mod_ref, g_ref)
    xo_ref[0] = xn
    f_ref[0] = f


def _mix0_out(o, pin, pool_w_bf, pool_scale, w_out_bf, x_all, mods, g4, n_lat_tiles, lat_len, ctx_len):
    bn, tt, d = x_all.shape
    nt = tt // TM
    attn_w = o.shape[2]
    pw = pin.shape[2]
    hb = TM // HALO
    n_hblk = tt // HALO
    mod_map = lambda b, t: (jnp.where(t < n_lat_tiles, b, bn), 0, 0)
    kern = functools.partial(_mix0_kernel, n_lat_tiles=n_lat_tiles, lat_len=lat_len, ctx_len=ctx_len)
    return pl.pallas_call(
        kern,
        grid=(bn, nt),
        in_specs=[
            pl.BlockSpec((1, TM, attn_w), lambda b, t: (b, t, 0)),
            pl.BlockSpec((1, TM, pw), lambda b, t: (b, t, 0)),
            pl.BlockSpec((1, HALO, pw), lambda b, t: (b, jnp.maximum(t * hb - 1, 0), 0)),
            pl.BlockSpec((1, HALO, pw), lambda b, t: (b, jnp.minimum((t + 1) * hb, n_hblk - 1), 0)),
            pl.BlockSpec(pool_w_bf.shape, lambda b, t: (0, 0, 0)),
            pl.BlockSpec((1, pw), lambda b, t: (0, 0)),
            pl.BlockSpec(w_out_bf.shape, lambda b, t: (0, 0)),
            pl.BlockSpec((1, TM, d), lambda b, t: (b, t, 0)),
            pl.BlockSpec((1, 6, d), mod_map),
            pl.BlockSpec((4, d), lambda b, t: (0, 0)),
        ],
        out_specs=[
            pl.BlockSpec((1, TM, d), lambda b, t: (b, t, 0)),
            pl.BlockSpec((1, TM, d), lambda b, t: (b, t, 0)),
        ],
        out_shape=[jax.ShapeDtypeStruct((bn, tt, d), F32), jax.ShapeDtypeStruct((bn, tt, d), F32)],
        scratch_shapes=[pltpu.VMEM((TM + 2 * HALO, pw), F32), pltpu.VMEM((TM, attn_w + pw), BF16)],
        compiler_params=_cparams("arbitrary", "arbitrary"),
        name="pool_outproj",
    )(o, pin, pin, pin, pool_w_bf, pool_scale.reshape(1, -1), w_out_bf, x_all, mods, g4)


def _router_kernel(f_ref, rw_ref, rb_ref, idx_ref, gate_ref, rank_ref, cnt_ref, carry_sc):
    i = pl.program_id(0)

    @pl.when(i == 0)
    def _():
        carry_sc[...] = jnp.zeros(carry_sc.shape, F32)

    lane = lax.broadcasted_iota(jnp.int32, (TM, LANES), 1).astype(F32)
    logits = _dot(f_ref[...].astype(BF16), rw_ref[...]) + rb_ref[...]
    l = jnp.where(lane < N_EXPERTS, logits, -jnp.inf)
    vals, sels, idxs = [], [], []
    for _ in range(TOP_K):
        m = jnp.max(l, axis=-1, keepdims=True)
        idx = jnp.min(jnp.where(l == m, lane, float(LANES)), axis=-1, keepdims=True)
        sel = lane == idx
        vals.append(m)
        idxs.append(idx)
        sels.append(sel)
        l = jnp.where(sel, -jnp.inf, l)
    es = [jnp.exp(v - vals[0]) for v in vals]
    denom = es[0] + es[1] + es[2] + es[3]
    onehot = jnp.zeros((TM, LANES), F32)
    for sel in sels:
        onehot = onehot + jnp.where(sel, 1.0, 0.0)
    r_i = lax.broadcasted_iota(jnp.int32, (TM, TM), 0)
    c_i = lax.broadcasted_iota(jnp.int32, (TM, TM), 1)
    tril = jnp.where(c_i < r_i, 1.0, 0.0).astype(BF16)
    before = _dot(tril, onehot.astype(BF16)) + carry_sc[0:1, :]
    idx_out = jnp.zeros((TM, LANES), F32)
    gate_out = jnp.zeros((TM, LANES), F32)
    rank_out = jnp.zeros((TM, LANES), F32)
    for k in range(TOP_K):
        rk = jnp.sum(jnp.where(sels[k], before, 0.0), axis=-1, keepdims=True)
        idx_out = jnp.where(lane == k, idxs[k], idx_out)
        gate_out = jnp.where(lane == k, es[k] / denom, gate_out)
        rank_out = jnp.where(lane == k, rk, rank_out)
    idx_ref[...] = idx_out.astype(jnp.int32)
    gate_ref[...] = gate_out
    rank_ref[...] = rank_out.astype(jnp.int32)
    carry_sc[...] = carry_sc[...] + jnp.sum(onehot, axis=0, keepdims=True)
    cnt_ref[...] = carry_sc[...].astype(jnp.int32)


def _router(f2, router_w, router_b):
    n, d = f2.shape
    rw = jnp.zeros((d, LANES), BF16).at[:, :N_EXPERTS].set(router_w.astype(BF16))
    rb = jnp.zeros((1, LANES), F32).at[0, :N_EXPERTS].set(router_b)
    tile = pl.BlockSpec((TM, LANES), lambda i: (i, 0))
    return pl.pallas_call(
        _router_kernel,
        grid=(n // TM,),
        in_specs=[
            pl.BlockSpec((TM, d), lambda i: (i, 0)),
            pl.BlockSpec((d, LANES), lambda i: (0, 0)),
            pl.BlockSpec((1, LANES), lambda i: (0, 0)),
        ],
        out_specs=[tile, tile, tile, pl.BlockSpec((HALO, LANES), lambda i: (0, 0))],
        out_shape=[
            jax.ShapeDtypeStruct((n, LANES), jnp.int32),
            jax.ShapeDtypeStruct((n, LANES), F32),
            jax.ShapeDtypeStruct((n, LANES), jnp.int32),
            jax.ShapeDtypeStruct((HALO, LANES), jnp.int32),
        ],
        scratch_shapes=[pltpu.VMEM((HALO, LANES), F32)],
        compiler_params=_cparams("arbitrary"),
        name="router",
    )(f2, rw, rb)


W_CHUNKS = 4


def _expert_kernel(bs_ref, nxt_ref, tok_ref, f_hbm, wg_hbm, wu_hbm, wd_hbm, bg_ref, bu_ref, bd_ref, y_hbm,
                   wg_bf, wu_bf, wd_bf, stage_a, stage_d, xbuf, ybuf, wsem_a, wsem_d, xsem, ysem,
                   *, layer, n_blocks):
    e = pl.program_id(0)
    n_exp = pl.num_programs(0)
    b0, b1, n_total = bs_ref[e], bs_ref[e + 1], bs_ref[n_exp]
    first_e = nxt_ref[n_exp]
    mats = ((wg_hbm, wg_bf, stage_a, wsem_a), (wu_hbm, wu_bf, stage_a, wsem_a), (wd_hbm, wd_bf, stage_d, wsem_d))
    n_chunks = len(mats) * W_CHUNKS

    def chunk_copy(ee, q):
        src, _, stage, sem = mats[q // W_CHUNKS]
        rows = stage.shape[1]
        c = q % W_CHUNKS
        return pltpu.make_async_copy(src.at[layer, ee, pl.ds(c * rows, rows)], stage.at[q % 2], sem.at[q % 2])

    def gather(blk, slot):
        base = blk * TE

        def body(r, carry):
            tok = tok_ref[base + r]
            pltpu.make_async_copy(f_hbm.at[pl.ds(tok, 1)], xbuf.at[slot, pl.ds(r, 1)], xsem.at[slot]).start()
            return carry

        lax.fori_loop(0, TE, body, 0, unroll=DMA_UNROLL)

    def y_copy(blk):
        return pltpu.make_async_copy(ybuf, y_hbm.at[pl.ds(blk * TE, TE)], ysem.at[0])

    @pl.when(e == first_e)
    def _():
        gather(0, 0)
        chunk_copy(e, 0).start()
        chunk_copy(e, 1).start()

    @pl.when(b1 > b0)
    def _():
        nxt = nxt_ref[e]
        for q in range(n_chunks):
            _, dst, stage, _ = mats[q // W_CHUNKS]
            rows = stage.shape[1]
            c = q % W_CHUNKS
            chunk_copy(e, q).wait()
            dst[c * rows:(c + 1) * rows, :] = stage[q % 2].astype(BF16)
            if q + 2 < n_chunks:
                chunk_copy(e, q + 2).start()
            else:
                @pl.when(nxt < n_exp)
                def _():
                    chunk_copy(nxt, q + 2 - n_chunks).start()

        def block(s, carry):
            slot = s % 2

            @pl.when(s + 1 < n_total)
            def _():
                gather(s + 1, 1 - slot)

            pltpu.make_async_copy(f_hbm.at[pl.ds(0, TE)], xbuf.at[slot], xsem.at[slot]).wait()
            x = xbuf[slot].astype(BF16)
            glu = jnp.minimum(_dot(x, wg_bf[...]) + bg_ref[0], SWIGLU_LIMIT)
            lin = jnp.clip(_dot(x, wu_bf[...]) + bu_ref[0], -SWIGLU_LIMIT, SWIGLU_LIMIT)
            act = glu * _sigmoid(SWIGLU_ALPHA * glu) * (lin + 1.0)
            y = _dot(act.astype(BF16), wd_bf[...]) + bd_ref[0]

            @pl.when(s > 0)
            def _():
                y_copy(s - 1).wait()

            ybuf[...] = y
            y_copy(s).start()
            return carry

        lax.fori_loop(b0, b1, block, 0)

    @pl.when(e == n_exp - 1)
    def _():
        y_copy(n_total - 1).wait()
        ybuf[...] = jnp.zeros(ybuf.shape, F32)

        def fill(s, carry):
            y_copy(s).start()
            y_copy(s).wait()
            return carry

        lax.fori_loop(n_total, n_blocks, fill, 0)


def _experts(f2, blk_start, next_e, slot_tok, layer, wg, bg, wu, bu, wd, bd, n_blocks):
    n, d = f2.shape
    _, ne, _, de = wg.shape
    bias = lambda w: pl.BlockSpec((1, 1, w), lambda e, bs, nx, tok: (e, 0, 0))
    hbm = pl.BlockSpec(memory_space=pl.ANY)
    grid_spec = pltpu.PrefetchScalarGridSpec(
        num_scalar_prefetch=3,
        grid=(ne,),
        in_specs=[hbm, hbm, hbm, hbm, bias(de), bias(de), bias(d)],
        out_specs=hbm,
        scratch_shapes=[
            pltpu.VMEM((d, de), BF16), pltpu.VMEM((d, de), BF16), pltpu.VMEM((de, d), BF16),
            pltpu.VMEM((2, d // W_CHUNKS, de), F32), pltpu.VMEM((2, de // W_CHUNKS, d), F32),
            pltpu.VMEM((2, TE, d), F32), pltpu.VMEM((TE, d), F32),
            pltpu.SemaphoreType.DMA((2,)), pltpu.SemaphoreType.DMA((2,)),
            pltpu.SemaphoreType.DMA((2,)), pltpu.SemaphoreType.DMA((1,)),
        ],
    )
    return pl.pallas_call(
        functools.partial(_expert_kernel, layer=layer, n_blocks=n_blocks),
        grid_spec=grid_spec,
        out_shape=jax.ShapeDtypeStruct((n_blocks * TE, d), F32),
        compiler_params=_cparams("arbitrary"),
        name="experts",
    )(blk_start, next_e, slot_tok, f2, wg, wu, wd, bg.reshape(ne, 1, de), bu.reshape(ne, 1, de), bd.reshape(ne, 1, d))


def _combine_kernel(dest_ref, tmod_ref, yb_hbm, gate_ref, x_ref, mod_ref, g_ref, xo_ref, buf, sem):
    i = pl.program_id(0)
    nt = pl.num_programs(0)

    def issue(tile, slot):
        base = tile * TM * TOP_K

        def body(r, carry):
            for k in range(TOP_K):
                dst = dest_ref[base + r * TOP_K + k]
                pltpu.make_async_copy(yb_hbm.at[pl.ds(dst, 1)], buf.at[slot, k, pl.ds(r, 1)], sem.at[slot]).start()
            return carry

        lax.fori_loop(0, TM, body, 0, unroll=DMA_UNROLL // 2)

    @pl.when(i == 0)
    def _():
        issue(0, 0)

    @pl.when(i + 1 < nt)
    def _():
        issue(i + 1, (i + 1) % 2)

    slot = i % 2
    for k in range(TOP_K):
        pltpu.make_async_copy(yb_hbm.at[pl.ds(0, TM)], buf.at[slot, k], sem.at[slot]).wait()
    gate = gate_ref[...]
    out = gate[:, 0:1] * buf[slot, 0]
    for k in range(1, TOP_K):
        out = out + gate[:, k:k + 1] * buf[slot, k]
    xo_ref[...] = x_ref[...] + mod_ref[0, 5:6, :] * _rms(out, g_ref[3:4, :])


def _combine(yb, dest, tile_mod, gates, x2, mods, g4):
    n, d = x2.shape
    grid_spec = pltpu.PrefetchScalarGridSpec(
        num_scalar_prefetch=2,
        grid=(n // TM,),
        in_specs=[
            pl.BlockSpec(memory_space=pl.ANY),
            pl.BlockSpec((TM, LANES), lambda i, dst, tm: (i, 0)),
            pl.BlockSpec((TM, d), lambda i, dst, tm: (i, 0)),
            pl.BlockSpec((1, 6, d), lambda i, dst, tm: (tm[i], 0, 0)),
            pl.BlockSpec((4, d), lambda i, dst, tm: (0, 0)),
        ],
        out_specs=pl.BlockSpec((TM, d), lambda i, dst, tm: (i, 0)),
        scratch_shapes=[pltpu.VMEM((2, TOP_K, TM, d), F32), pltpu.SemaphoreType.DMA((2,))],
    )
    return pl.pallas_call(
        _combine_kernel,
        grid_spec=grid_spec,
        out_shape=jax.ShapeDtypeStruct((n, d), F32),
        compiler_params=_cparams("arbitrary"),
        name="moe_combine",
    )(dest, tile_mod, yb, gates, x2, mods, g4)


def _moe(f2, x2, tile_mod, mods, g4, router_w, router_b, layer, wg, bg, wu, bu, wd, bd):
    n, d = f2.shape
    idx, gates, rank, cnt = _router(f2, router_w, router_b)
    idx = idx[:, :TOP_K]
    rank = rank[:, :TOP_K]
    counts = cnt[0, :N_EXPERTS]
    padded = (counts + TE - 1) // TE * TE
    pend = jnp.cumsum(padded)
    pstart = pend - padded
    dest = (pstart[idx] + rank).reshape(-1).astype(jnp.int32)
    n_asg = n * TOP_K
    n_blocks = -(-(n_asg + N_EXPERTS * (TE - 1)) // TE)
    blk_start = jnp.concatenate([jnp.zeros((1,), jnp.int32), (pend // TE).astype(jnp.int32)])
    eid = jnp.where(counts > 0, jnp.arange(N_EXPERTS, dtype=jnp.int32), N_EXPERTS)
    first_from = lax.cummin(eid, axis=0, reverse=True)
    next_e = jnp.concatenate([first_from[1:], jnp.full((1,), N_EXPERTS, jnp.int32), first_from[:1]])
    slot_tok = jnp.zeros((n_blocks * TE,), jnp.int32).at[dest].set(jnp.repeat(jnp.arange(n, dtype=jnp.int32), TOP_K))
    yb = _experts(f2, blk_start, next_e, slot_tok, layer, wg, bg, wu, bu, wd, bd, n_blocks)
    return _combine(yb, dest, tile_mod, gates, x2, mods, g4)


def _normmod_kernel(x_ref, mod_ref, g_ref, h_ref):
    h_ref[0] = _rms(x_ref[0], g_ref[0:1, :]) * (1.0 + mod_ref[0, 1:2, :]) + mod_ref[0, 0:1, :]


def _normmod(x_all, mods, g4, n_lat_tiles):
    bn, tt, d = x_all.shape
    mod_map = lambda b, t: (jnp.where(t < n_lat_tiles, b, bn), 0, 0)
    tile = pl.BlockSpec((1, TM, d), lambda b, t: (b, t, 0))
    return pl.pallas_call(
        _normmod_kernel,
        grid=(bn, tt // TM),
        in_specs=[tile, pl.BlockSpec((1, 6, d), mod_map), pl.BlockSpec((4, d), lambda b, t: (0, 0))],
        out_specs=tile,
        out_shape=jax.ShapeDtypeStruct((bn, tt, d), F32),
        compiler_params=_cparams("arbitrary", "arbitrary"),
        name="norm_mod",
    )(x_all, mods, g4)


def _centred_shift(h, hprev_ref, hnext_ref, t, n_lat_tiles, n_tiles):
    seq_first = jnp.logical_or(t == 0, t == n_lat_tiles)
    seq_last = jnp.logical_or(t == n_lat_tiles - 1, t == n_tiles - 1)
    prev_row = jnp.where(seq_first, 0.0, hprev_ref[0, HALO - 1:HALO, :])
    next_row = jnp.where(seq_last, 0.0, hnext_ref[0, 0:1, :])
    tm = h.shape[0]
    row = lax.broadcasted_iota(jnp.int32, (tm, 1), 0)
    prev = jnp.where(row == 0, prev_row, pltpu.roll(h, 1, 0))
    nxt = jnp.where(row == tm - 1, next_row, pltpu.roll(h, tm - 1, 0))
    return 0.5 * (prev + nxt) - h


def _halo_specs(d, tt, tm):
    hb = tm // HALO
    n_hblk = tt // HALO
    return [
        pl.BlockSpec((1, tm, d), lambda *a: (a[-2], a[-1], 0)),
        pl.BlockSpec((1, HALO, d), lambda *a: (a[-2], jnp.maximum(a[-1] * hb - 1, 0), 0)),
        pl.BlockSpec((1, HALO, d), lambda *a: (a[-2], jnp.minimum((a[-1] + 1) * hb, n_hblk - 1), 0)),
    ]


def _shiftproj_kernel(h_ref, hprev_ref, hnext_ref, mix_ref, w_ref, o_ref, *, n_lat_tiles):
    t = pl.program_id(2)
    h = h_ref[0]
    xx = _centred_shift(h, hprev_ref, hnext_ref, t, n_lat_tiles, pl.num_programs(2))
    o_ref[0, 0] =_dot((h + xx * mix_ref[0]).astype(BF16), w_ref[0])


def _shiftproj(h, mix3, w3_bf, n_lat_tiles):
    bn, tt, d = h.shape
    nj = w3_bf.shape[0]
    kern = functools.partial(_shiftproj_kernel, n_lat_tiles=n_lat_tiles)
    return pl.pallas_call(
        kern,
        grid=(nj, bn, tt // TM),
        in_specs=_halo_specs(d, tt, TM) + [
            pl.BlockSpec((1, 1, d), lambda j, b, t: (j, 0, 0)),
            pl.BlockSpec((1, d, d), lambda j, b, t: (j, 0, 0)),
        ],
        out_specs=pl.BlockSpec((1, 1, TM, d), lambda j, b, t: (j, b, t, 0)),
        out_shape=jax.ShapeDtypeStruct((nj, bn, tt, d), F32),
        compiler_params=_cparams("arbitrary", "arbitrary", "arbitrary"),
        name="shift_proj",
    )(h, h, h, mix3, w3_bf)


def _softplus(x):
    return jnp.maximum(x, 0.0) + jnp.log(1.0 + jnp.exp(-jnp.abs(x)))


def _lora_kernel(h_ref, hprev_ref, hnext_ref, mix_ref, w1_ref, a1_ref, g1_ref, w2_ref, a2_ref, g2_ref,
                 w0_ref, a0_ref, dec_ref, aa_ref, gate_ref, *, n_lat_tiles):
    t = pl.program_id(1)
    h = h_ref[0]
    xx = _centred_shift(h, hprev_ref, hnext_ref, t, n_lat_tiles, pl.num_programs(1))
    hw = jnp.tanh(_dot((h + xx * mix_ref[0:1, :]).astype(BF16), w1_ref[...]))
    ha = _dot((h + xx * mix_ref[1:2, :]).astype(BF16), a1_ref[...])
    hg = _sigmoid(_dot((h + xx * mix_ref[2:3, :]).astype(BF16), g1_ref[...]))
    gate_ref[0] = _dot(hg.astype(BF16), g2_ref[...])
    for dd in range(2):
        sl = slice(dd * LORA_PAD, (dd + 1) * LORA_PAD)
        z = w0_ref[dd] + _dot(hw[:, sl].astype(BF16), w2_ref[dd])
        w_log = -_softplus(-z) - 0.5
        dec_ref[dd, 0] = jnp.exp(-jnp.exp(w_log))
        aa_ref[dd, 0] = _sigmoid(a0_ref[dd] + _dot(ha[:, sl].astype(BF16), a2_ref[dd]))


def _pad_lora(w_in, w_out):
    nd, d, r = w_in.shape
    wi = jnp.zeros((d, nd * LORA_PAD), BF16)
    wo = jnp.zeros((nd, LORA_PAD, d), BF16)
    for dd in range(nd):
        wi = wi.at[:, dd * LORA_PAD:dd * LORA_PAD + r].set(w_in[dd].astype(BF16))
        wo = wo.at[dd, :r].set(w_out[dd].astype(BF16))
    return wi, wo


def _lora(h, mix3, w1p, w2p, a1p, a2p, g1, g2, w0, a0, n_lat_tiles):
    bn, tt, d = h.shape
    const2 = lambda s: pl.BlockSpec(s, lambda b, t: (0, 0))
    const3 = lambda s: pl.BlockSpec(s, lambda b, t: (0, 0, 0))
    kern = functools.partial(_lora_kernel, n_lat_tiles=n_lat_tiles * (TM // TL))
    dir_out = pl.BlockSpec((2, 1, TL, d), lambda b, t: (0, b, t, 0))
    tok_out = pl.BlockSpec((1, TL, d), lambda b, t: (b, t, 0))
    dir_shape = jax.ShapeDtypeStruct((2, bn, tt, d), F32)
    tok_shape = jax.ShapeDtypeStruct((bn, tt, d), F32)
    return pl.pallas_call(
        kern,
        grid=(bn, tt // TL),
        in_specs=_halo_specs(d, tt, TL) + [
            const2((3, d)), const2(w1p.shape), const2(a1p.shape), const2((d, g1.shape[1])),
            const3(w2p.shape), const3(a2p.shape), const2(g2.shape),
            const3((2, 1, d)), const3((2, 1, d)),
        ],
        out_specs=[dir_out, dir_out, tok_out],
        out_shape=[dir_shape, dir_shape, tok_shape],
        compiler_params=_cparams("arbitrary", "arbitrary"),
        name="rwkv_lora",
    )(h, h, h, mix3, w1p, a1p, g1, w2p, a2p, g2, w0.reshape(2, 1, d), a0.reshape(2, 1, d))


N_CH = RWKV_HEAD
N_RH = 32
N_GRP = LANES // N_RH


def _colperm_kernel(w_ref, p_ref, o_ref):
    o_ref[...] = _dot(w_ref[...].astype(BF16), p_ref[...]).astype(BF16)


def _colperm(w, p):
    r, d = w.shape
    tr = min(r, 512)
    return pl.pallas_call(
        _colperm_kernel,
        grid=(r // tr,),
        in_specs=[pl.BlockSpec((tr, d), lambda i: (i, 0)), pl.BlockSpec((d, d), lambda i: (0, 0))],
        out_specs=pl.BlockSpec((tr, d), lambda i: (i, 0)),
        out_shape=jax.ShapeDtypeStruct((r, d), BF16),
        compiler_params=_cparams("arbitrary"),
        name="weight_colperm",
    )(w, p)


def _rowperm_kernel(pt_ref, w_ref, o_ref):
    o_ref[...] = _dot(pt_ref[...], w_ref[...].astype(BF16)).astype(BF16)


def _rowperm(w, pt):
    d, n = w.shape
    tn = min(n, 512)
    return pl.pallas_call(
        _rowperm_kernel,
        grid=(n // tn,),
        in_specs=[pl.BlockSpec((d, d), lambda i: (0, 0)), pl.BlockSpec((d, tn), lambda i: (0, i))],
        out_specs=pl.BlockSpec((d, tn), lambda i: (0, i)),
        out_shape=jax.ShapeDtypeStruct((d, n), BF16),
        compiler_params=_cparams("arbitrary"),
        name="weight_rowperm",
    )(pt, w)


def _lane_group(rows):
    return lax.broadcasted_iota(jnp.int32, (rows, LANES), 1) // N_RH


def _pick_groups(pieces, lg):
    out = pieces[N_GRP - 1]
    for g in range(N_GRP - 2, -1, -1):
        out = jnp.where(lg == g, pieces[g], out)
    return out


def _to_chain_blocks(srcs, j, lg):
    blks = [s[:, j * LANES:(j + 1) * LANES] for s in srcs]
    rolled = []
    for sh in range(N_GRP):
        merged = _pick_groups([blks[(i + sh) % N_GRP] for i in range(N_GRP)], lg)
        rolled.append(merged if sh == 0 else pltpu.roll(merged, sh * N_RH, 1))
    return [_pick_groups([rolled[(g - i) % N_GRP] for g in range(N_GRP)], lg) for i in range(N_GRP)]


def _head_sum(x):
    s = x[:, 0:LANES]
    for j in range(1, x.shape[1] // LANES):
        s = s + x[:, j * LANES:(j + 1) * LANES]
    s = s + pltpu.roll(s, 2 * N_RH, 1)
    return s + pltpu.roll(s, N_RH, 1)


def _prep_kernel(rkv_ref, dec_ref, aa_ref, kk_ref, ka_ref, w_ref, kd_ref, bb_ref, nk_ref, v_ref, r_ref):
    rows = rkv_ref.shape[2]
    bn = rkv_ref.shape[1]
    lg = _lane_group(rows)
    nblk = kk_ref.shape[1] // LANES
    ks, kks = [], []
    for b in range(bn):
        k = rkv_ref[1, b]
        kk = k * kk_ref[...]
        inv = lax.rsqrt(jnp.maximum(_head_sum(kk * kk), 1e-24))
        ks.append(k)
        kks.append(kk * jnp.concatenate([inv] * nblk, axis=1))
    groups = [(dd, b) for dd in range(2) for b in range(bn)]
    kd_src = [ks[b] * (1.0 + (aa_ref[dd, b] - 1.0) * ka_ref[...]) for dd, b in groups]
    bb_src = [kks[b] * aa_ref[dd, b] for dd, b in groups]
    w_src = [dec_ref[dd, b] for dd, b in groups]
    nk_src = [-kks[b] for _, b in groups]
    v_src = [rkv_ref[2, b] for _, b in groups]
    r_src = [rkv_ref[0, b] for _, b in groups]
    for j in range(N_CH // N_GRP):
        for srcs, ref in ((w_src, w_ref), (kd_src, kd_ref), (bb_src, bb_ref), (nk_src, nk_ref),
                          (v_src, v_ref), (r_src, r_ref)):
            for i, blk in enumerate(_to_chain_blocks(srcs, j, lg)):
                c = j * N_GRP + i
                ref[:, c * LANES:(c + 1) * LANES] = blk


def _prep(rkv, dec, aa, k_k, k_a):
    _, bn, tt, d = rkv.shape
    assert bn * 2 == N_GRP and d == N_CH * N_RH
    out = pl.BlockSpec((TP, N_CH * LANES), lambda t: (t, 0))
    shape = jax.ShapeDtypeStruct((tt, N_CH * LANES), F32)
    return pl.pallas_call(
        _prep_kernel,
        grid=(tt // TP,),
        in_specs=[
            pl.BlockSpec((3, bn, TP, d), lambda t: (0, 0, t, 0)),
            pl.BlockSpec((2, bn, TP, d), lambda t: (0, 0, t, 0)),
            pl.BlockSpec((2, bn, TP, d), lambda t: (0, 0, t, 0)),
            pl.BlockSpec((1, d), lambda t: (0, 0)),
            pl.BlockSpec((1, d), lambda t: (0, 0)),
        ],
        out_specs=[out] * 6,
        out_shape=[shape] * 6,
        compiler_params=_cparams("arbitrary"),
        name="rwkv_prep",
    )(rkv, dec, aa, k_k.reshape(1, d), k_a.reshape(1, d))


def _scan_kernel(*refs, steps):
    fwd_refs, rev_refs = refs[0:6], refs[6:12]
    yf_ref, yr_ref, s_sc, op_sc, sa_sc = refs[12:17]
    W, K, B, A, V, R, WR = range(7)
    n_oct = N_CH // HALO
    assert steps % 2 == 0

    @pl.when(pl.program_id(0) == 0)
    def _():
        s_sc[...] = jnp.zeros(s_sc.shape, F32)
        op_sc[...] = jnp.zeros(op_sc.shape, F32)
        sa_sc[...] = jnp.zeros(sa_sc.shape, F32)

    is_fwd = lax.broadcasted_iota(jnp.int32, (1, LANES), 1) < LANES // 2
    octet = lambda j: slice(j * HALO, (j + 1) * HALO)

    def step(i, carry):
        ir = steps - 1 - i
        cur = i % 2
        prv = 1 - cur
        for n, (f, g) in enumerate(zip(fwd_refs, rev_refs)):
            op_sc[cur, n] = jnp.where(is_fwd, f[i], g[ir])
        r = op_sc[cur, R]
        op_sc[cur, WR] = op_sc[cur, W] * r
        br = jnp.sum(op_sc[cur, B] * r, axis=0, keepdims=True)
        kr = jnp.sum(op_sc[cur, K] * r, axis=0, keepdims=True)
        sa_p = [sa_sc[octet(j), :] for j in range(n_oct)]
        vv_p = [op_sc[prv, V, octet(j), :] for j in range(n_oct)]

        def sweep(g, acc):
            sa, y0 = list(acc[:n_oct]), list(acc[n_oct:])
            for u in range(SCAN_UNROLL):
                kk = g * SCAN_UNROLL + u
                w_row = op_sc[prv, W, pl.ds(kk, 1), :]
                b_row = op_sc[prv, B, pl.ds(kk, 1), :]
                k_row = op_sc[prv, K, pl.ds(kk, 1), :]
                a_row = op_sc[cur, A, pl.ds(kk, 1), :]
                wr_row = op_sc[cur, WR, pl.ds(kk, 1), :]
                for j in range(n_oct):
                    s = s_sc[kk, octet(j), :] * w_row + sa_p[j] * b_row + vv_p[j] * k_row
                    s_sc[kk, octet(j), :] = s
                    sa[j] = sa[j] + s * a_row
                    y0[j] = y0[j] + s * wr_row
            return tuple(sa) + tuple(y0)

        zero = jnp.zeros((HALO, LANES), F32)
        acc = lax.fori_loop(0, N_CH // SCAN_UNROLL, sweep, (zero,) * (2 * n_oct))
        for j in range(n_oct):
            sa_j, y0_j = acc[j], acc[n_oct + j]
            sa_sc[octet(j), :] = sa_j
            y = y0_j + sa_j * br + op_sc[cur, V, octet(j), :] * kr
            yf_ref[i, octet(j), :] = y
            yr_ref[ir, octet(j), :] = y
        return carry

    lax.fori_loop(0, steps, step, 0)


def _scan(ops, lat_len, ctx_len, steps=32):
    tt = ops[0].shape[0]
    n_lat, n_ctx = lat_len // steps, ctx_len // steps
    ops = [x.reshape(tt, N_CH, LANES) for x in ops]

    def fwd_blk(i):
        return jnp.where(i < n_ctx, n_lat + i, i - n_ctx)

    def rev_blk(i):
        return jnp.where(i < n_ctx, n_lat + n_ctx - 1 - i, n_lat - 1 - (i - n_ctx))

    fwd = pl.BlockSpec((steps, N_CH, LANES), lambda i: (fwd_blk(i), 0, 0))
    rev = pl.BlockSpec((steps, N_CH, LANES), lambda i: (rev_blk(i), 0, 0))
    shape = jax.ShapeDtypeStruct((tt, N_CH, LANES), F32)
    return pl.pallas_call(
        functools.partial(_scan_kernel, steps=steps),
        grid=(tt // steps,),
        in_specs=[fwd] * 6 + [rev] * 6,
        out_specs=[fwd, rev],
        out_shape=[shape, shape],
        scratch_shapes=[pltpu.VMEM((N_CH, N_CH, LANES), F32), pltpu.VMEM((2, 7, N_CH, LANES), F32),
                        pltpu.VMEM((N_CH, LANES), F32)],
        compiler_params=_cparams("arbitrary"),
        name="rwkv_scan",
    )(*ops, *ops)


def _readout_kernel(yf_ref, yr_ref, r_ref, kd_ref, v_ref, gate_ref, lnw_ref, lnb_ref, rk_ref, wo_ref,
                    x_ref, mod_ref, g_ref, xo_ref, f_ref, o_sc):
    rows = yf_ref.shape[0]
    bn = x_ref.shape[0]
    half = LANES // 2
    lg = _lane_group(rows)
    blk = lambda ref, c: ref[:, c * LANES:(c + 1) * LANES]
    ys = [blk(yf_ref, c) + pltpu.roll(blk(yr_ref, c), half, 1) for c in range(N_CH)]
    inv_n = 1.0 / N_CH
    mu = ys[0]
    for c in range(1, N_CH):
        mu = mu + ys[c]
    mu = mu * inv_n
    var = (ys[0] - mu) * (ys[0] - mu)
    for c in range(1, N_CH):
        var = var + (ys[c] - mu) * (ys[c] - mu)
    rstd = lax.rsqrt(var * inv_n + GN_EPS)
    s = blk(r_ref, 0) * blk(rk_ref, 0) * blk(kd_ref, 0)
    for c in range(1, N_CH):
        s = s + blk(r_ref, c) * blk(rk_ref, c) * blk(kd_ref, c)
    s = s + pltpu.roll(s, half, 1)
    for c in range(N_CH):
        o_sc[:, c * LANES:(c + 1) * LANES] = ((ys[c] - mu) * rstd * blk(lnw_ref, c) + blk(lnb_ref, c)
                                              + s * blk(v_ref, c))
    for b in range(bn):
        cols = []
        for j in range(N_CH // N_GRP):
            pieces = []
            for i in range(N_GRP):
                c = j * N_GRP + i
                shift = ((i - b) % N_GRP) * N_RH
                piece = o_sc[:, c * LANES:(c + 1) * LANES]
                pieces.append(piece if shift == 0 else pltpu.roll(piece, shift, 1))
            cols.append(_pick_groups(pieces, lg))
        o = jnp.concatenate(cols, axis=1) * gate_ref[b]
        yo = _dot(o.astype(BF16), wo_ref[...])
        xn = x_ref[b] + mod_ref[b, 2:3, :] * _rms(yo, g_ref[1:2, :])
        xo_ref[b] = xn
        f_ref[b] = _rms(xn, g_ref[2:3, :]) * (1.0 + mod_ref[b, 4:5, :]) + mod_ref[b, 3:4, :]


def _chain_param(p):
    pc = p.reshape(N_RH, N_CH).T
    return jnp.tile(pc, (1, N_GRP)).reshape(1, N_CH * LANES)


def _readout(yf, yr, r_c, kd_c, v_c, gate, ln_w, ln_b, r_k, wo_bf, x_all, mods, g4, lat_len):
    bn, tt, d = x_all.shape
    cw = N_CH * LANES
    chain = pl.BlockSpec((TP, cw), lambda t: (t, 0))
    tok = pl.BlockSpec((bn, TP, d), lambda t: (0, t, 0))
    row = pl.BlockSpec((1, cw), lambda t: (0, 0))
    out_shape = jax.ShapeDtypeStruct((bn, lat_len, d), F32)
    return pl.pallas_call(
        _readout_kernel,
        grid=(lat_len // TP,),
        in_specs=[
            chain, chain, chain, chain, chain, tok, row, row, row,
            pl.BlockSpec((d, d), lambda t: (0, 0)),
            tok,
            pl.BlockSpec((bn + 1, 6, d), lambda t: (0, 0, 0)),
            pl.BlockSpec((4, d), lambda t: (0, 0)),
        ],
        out_specs=[tok, tok],
        out_shape=[out_shape, out_shape],
        scratch_shapes=[pltpu.VMEM((TP, cw), F32)],
        compiler_params=_cparams("arbitrary"),
        name="rwkv_readout",
    )(yf.reshape(tt, cw), yr.reshape(tt, cw), r_c, kd_c, v_c, gate, _chain_param(ln_w), _chain_param(ln_b),
      _chain_param(r_k), wo_bf, x_all, mods, g4)


def _rope_tables(lat_len, ctx_len):
    m = HEAD_DIM // 4
    t = jnp.arange(lat_len)
    inv = ROPE_THETA ** (-jnp.arange(0, 2 * m, 2, dtype=F32) / (2 * m))
    ar = (t // GRID_W).astype(F32)[:, None] * inv[None, :]
    ac = (t % GRID_W).astype(F32)[:, None] * inv[None, :]
    cos = jnp.concatenate([jnp.cos(ar), jnp.cos(ar), jnp.cos(ac), jnp.cos(ac)], axis=-1)
    sin = jnp.concatenate([-jnp.sin(ar), jnp.sin(ar), -jnp.sin(ac), jnp.sin(ac)], axis=-1)
    cos = jnp.concatenate([cos, jnp.ones((ctx_len, HEAD_DIM), F32)], axis=0)
    sin = jnp.concatenate([sin, jnp.zeros((ctx_len, HEAD_DIM), F32)], axis=0)
    return cos, sin


def kernel(x, c, ctx, c_ctx, w_mod, b_mod, norm_g, ab_w_in, ab_q_gain, ab_k_gain, pool_w, pool_scale, ab_w_out, rw_mix, rw_wr, rw_wk, rw_wv, rw_wo, rw_w0, rw_w1, rw_w2, rw_a0, rw_a1, rw_a2, rw_g1, rw_g2, rw_k_k, rw_k_a, rw_r_k, rw_ln_w, rw_ln_b, router_w, router_b, moe_w_gate, moe_b_gate, moe_w_up, moe_b_up, moe_w_down, moe_b_down):
    bn, lat_len, d = x.shape
    ctx_len = ctx.shape[1]
    assert lat_len % TM == 0 and ctx_len == TM and d % LANES == 0
    n_lat_tiles = lat_len // TM
    tt = lat_len + ctx_len
    nt = tt // TM

    cvec = jnp.zeros((HALO, d), F32).at[:bn].set(c).at[bn].set(c_ctx)
    mods_all = _mod_rows(cvec, w_mod, b_mod)
    mods = [mods_all[i, :bn + 1].reshape(bn + 1, 6, d) for i in range(w_mod.shape[0])]

    x_all = jnp.concatenate([x, ctx], axis=1)

    cos_t, sin_t = _rope_tables(lat_len, ctx_len)
    q, k, v, pin = _inproj(x_all, mods[0], norm_g[0], ab_w_in[0].astype(BF16), ab_q_gain[0], ab_k_gain[0],
                           cos_t, sin_t, n_lat_tiles)
    o = _attention(q, k, v, n_lat_tiles)
    x_all, f = _mix0_out(o, pin, pool_w[0].astype(BF16), pool_scale[0], ab_w_out[0].astype(BF16), x_all,
                         mods[0], norm_g[0], n_lat_tiles, lat_len, ctx_len)
    tile_mod = jnp.where(jnp.arange(bn * nt) % nt < n_lat_tiles, jnp.arange(bn * nt) // nt, bn).astype(jnp.int32)
    x_all = _moe(f.reshape(bn * tt, d), x_all.reshape(bn * tt, d), tile_mod, mods[0], norm_g[0],
                 router_w[0], router_b[0], 0, moe_w_gate, moe_b_gate[0], moe_w_up, moe_b_up[0],
                 moe_w_down, moe_b_down[0]).reshape(bn, tt, d)

    h = _normmod(x_all, mods[1], norm_g[1], n_lat_tiles)
    mix = rw_mix[0]
    perm = jnp.arange(d).reshape(N_RH, N_CH).T.reshape(-1)
    pmat = (jnp.arange(d)[:, None] == perm[None, :]).astype(BF16)
    w3 = _colperm(jnp.concatenate([rw_wr[0], rw_wk[0], rw_wv[0]], axis=0), pmat).reshape(3, d, d)
    rkv = _shiftproj(h, jnp.stack([mix[0], mix[2], mix[3]])[:, None, :], w3, n_lat_tiles)
    w1p, w2p = _pad_lora(rw_w1[0], rw_w2[0])
    a1p, a2p = _pad_lora(rw_a1[0], rw_a2[0])
    w2p = _colperm(w2p.reshape(2 * LORA_PAD, d), pmat).reshape(2, LORA_PAD, d)
    a2p = _colperm(a2p.reshape(2 * LORA_PAD, d), pmat).reshape(2, LORA_PAD, d)
    dec, aa, gate = _lora(h, jnp.stack([mix[1], mix[4], mix[5]]), w1p, w2p, a1p, a2p, rw_g1[0].astype(BF16),
                          _colperm(rw_g2[0], pmat), rw_w0[0][:, perm], rw_a0[0][:, perm], n_lat_tiles)
    w_c, kd_c, bb_c, nk_c, v_c, r_c = _prep(rkv, dec, aa, rw_k_k[0][perm], rw_k_a[0][perm])
    yf, yr = _scan([w_c, kd_c, bb_c, nk_c, v_c, r_c], lat_len, ctx_len)
    x_lat, f = _readout(yf, yr, r_c, kd_c, v_c, gate, rw_ln_w[0], rw_ln_b[0], rw_r_k[0].reshape(-1),
                        _rowperm(rw_wo[0], (perm[:, None] == jnp.arange(d)[None, :]).astype(BF16)), x_all,
                        mods[1], norm_g[1], lat_len)
    tile_mod = (jnp.arange(bn * n_lat_tiles) // n_lat_tiles).astype(jnp.int32)
    out = _moe(f.reshape(bn * lat_len, d), x_lat.reshape(bn * lat_len, d), tile_mod, mods[1], norm_g[1],
               router_w[1], router_b[1], 1, moe_w_gate, moe_b_gate[1], moe_w_up, moe_b_up[1],
               moe_w_down, moe_b_down[1])
    return out.reshape(bn, lat_len, d)
```

```python
import functools

import jax
import jax.numpy as jnp
from jax import lax
from jax.experimental import pallas as pl
from jax.experimental.pallas import tpu as pltpu

F32 = jnp.float32
BF16 = jnp.bfloat16

TM = 256
TL = 128
TP = 64
TE = 256
SCAN_UNROLL = 16
DMA_UNROLL = 8
LANES = 128
HALO = 8
VMEM_LIMIT = 56 * 1024 * 1024

RMS_EPS = 1e-6
GN_EPS = 64e-5
HEAD_DIM = 128
N_HEADS = 8
N_KV_HEADS = 2
GQA_GROUP = N_HEADS // N_KV_HEADS
GRID_W = 64
ROPE_THETA = 10000.0
POOL_WINDOWS = (2, 4, 8, 16)
RWKV_HEAD = 64
LORA_PAD = 128
N_EXPERTS = 32
TOP_K = 4
SWIGLU_LIMIT = 7.0
SWIGLU_ALPHA = 1.702


def _cparams(*sem):
    return pltpu.CompilerParams(dimension_semantics=sem, vmem_limit_bytes=VMEM_LIMIT)


def _rms(x, g):
    return x * lax.rsqrt(jnp.mean(x * x, axis=-1, keepdims=True) + RMS_EPS) * g


def _sigmoid(x):
    return 1.0 / (1.0 + jnp.exp(-x))


def _dot(a, b):
    return jnp.dot(a, b, preferred_element_type=F32)


def _mod_kernel(c_ref, w_ref, b_ref, o_ref):
    c = c_ref[...]
    s = c * _sigmoid(c)
    o_ref[0] = _dot(s.astype(BF16), w_ref[0].astype(BF16)) + b_ref[0]


def _mod_rows(cvec, w_mod, b_mod):
    depth, d, n = w_mod.shape
    tn = 1024
    rows = cvec.shape[0]
    return pl.pallas_call(
        _mod_kernel,
        grid=(depth, n // tn),
        in_specs=[
            pl.BlockSpec((rows, d), lambda i, j: (0, 0)),
            pl.BlockSpec((1, d, tn), lambda i, j: (i, 0, j)),
            pl.BlockSpec((1, 1, tn), lambda i, j: (i, 0, j)),
        ],
        out_specs=pl.BlockSpec((1, rows, tn), lambda i, j: (i, 0, j)),
        out_shape=jax.ShapeDtypeStruct((depth, rows, n), F32),
        compiler_params=_cparams("arbitrary", "arbitrary"),
        name="mod_rows",
    )(cvec, w_mod, b_mod.reshape(depth, 1, n))


def _inproj_kernel(x_ref, mod_ref, g_ref, w_ref, qg_ref, kg_ref, cos_ref, sin_ref,
                   q_ref, k_ref, v_ref, pin_ref):
    x = x_ref[0]
    h = _rms(x, g_ref[0:1, :]) * (1.0 + mod_ref[0, 1:2, :]) + mod_ref[0, 0:1, :]
    u = _dot(h.astype(BF16), w_ref[...])
    cos = cos_ref[...]
    sin = sin_ref[...]
    lane = lax.broadcasted_iota(jnp.int32, (TM, HEAD_DIM), 1)
    first_half = (lane & 32) == 0

    def head(xh, gain, scale):
        y = _rms(xh, gain)
        sw = jnp.where(first_half, pltpu.roll(y, 96, 1), pltpu.roll(y, 32, 1))
        return ((y * cos + sw * sin) * scale).astype(BF16)

    attn_w = N_HEADS * HEAD_DIM
    kv_w = N_KV_HEADS * HEAD_DIM
    for hh in range(N_HEADS):
        sl = slice(hh * HEAD_DIM, (hh + 1) * HEAD_DIM)
        q_ref[0, :, sl] = head(u[:, sl], qg_ref[...], HEAD_DIM ** -0.5)
    for hh in range(N_KV_HEADS):
        sl = slice(hh * HEAD_DIM, (hh + 1) * HEAD_DIM)
        k_ref[0, :, sl] = head(u[:, attn_w + hh * HEAD_DIM:attn_w + (hh + 1) * HEAD_DIM], kg_ref[...], 1.0)
    v_ref[0] = u[:, attn_w + kv_w:attn_w + 2 * kv_w].astype(BF16)
    pin_ref[0] = u[:, attn_w + 2 * kv_w:]


def _inproj(x_all, mods, g4, w_in_bf, q_gain, k_gain, cos_t, sin_t, n_lat_tiles):
    bn, tt, d = x_all.shape
    nt = tt // TM
    attn_w = N_HEADS * HEAD_DIM
    kv_w = N_KV_HEADS * HEAD_DIM
    pool_w = w_in_bf.shape[1] - attn_w - 2 * kv_w
    mod_map = lambda b, t: (jnp.where(t < n_lat_tiles, b, bn), 0, 0)
    return pl.pallas_call(
        _inproj_kernel,
        grid=(bn, nt),
        in_specs=[
            pl.BlockSpec((1, TM, d), lambda b, t: (b, t, 0)),
            pl.BlockSpec((1, 6, d), mod_map),
            pl.BlockSpec((4, d), lambda b, t: (0, 0)),
            pl.BlockSpec(w_in_bf.shape, lambda b, t: (0, 0)),
            pl.BlockSpec((1, HEAD_DIM), lambda b, t: (0, 0)),
            pl.BlockSpec((1, HEAD_DIM), lambda b, t: (0, 0)),
            pl.BlockSpec((TM, HEAD_DIM), lambda b, t: (t, 0)),
            pl.BlockSpec((TM, HEAD_DIM), lambda b, t: (t, 0)),
        ],
        out_specs=[
            pl.BlockSpec((1, TM, attn_w), lambda b, t: (b, t, 0)),
            pl.BlockSpec((1, TM, kv_w), lambda b, t: (b, t, 0)),
            pl.BlockSpec((1, TM, kv_w), lambda b, t: (b, t, 0)),
            pl.BlockSpec((1, TM, pool_w), lambda b, t: (b, t, 0)),
        ],
        out_shape=[
            jax.ShapeDtypeStruct((bn, tt, attn_w), BF16),
            jax.ShapeDtypeStruct((bn, tt, kv_w), BF16),
            jax.ShapeDtypeStruct((bn, tt, kv_w), BF16),
            jax.ShapeDtypeStruct((bn, tt, pool_w), F32),
        ],
        compiler_params=_cparams("arbitrary", "arbitrary"),
        name="inproj",
    )(x_all, mods, g4, w_in_bf, q_gain.reshape(1, -1), k_gain.reshape(1, -1), cos_t, sin_t)


def _attn_kernel(q_ref, k_ref, v_ref, o_ref, *, n_lat_tiles, lat_len):
    t = pl.program_id(2)

    def attend(kk, vv):
        for i in range(GQA_GROUP):
            sl = slice(i * HEAD_DIM, (i + 1) * HEAD_DIM)
            s = lax.dot_general(q_ref[0, :, sl], kk, (((1,), (1,)), ((), ())), preferred_element_type=F32)
            p = jnp.exp(s - jnp.max(s, axis=-1, keepdims=True))
            l = jnp.sum(p, axis=-1, keepdims=True)
            o_ref[0, :, sl] = (_dot(p.astype(BF16), vv) / l).astype(BF16)

    @pl.when(t < n_lat_tiles)
    def _():
        attend(k_ref[0], v_ref[0])

    @pl.when(t >= n_lat_tiles)
    def _():
        attend(k_ref[0, lat_len:, :], v_ref[0, lat_len:, :])


def _attention(q, k, v, n_lat_tiles):
    bn, tt, attn_w = q.shape
    nt = tt // TM
    gw = GQA_GROUP * HEAD_DIM
    kern = functools.partial(_attn_kernel, n_lat_tiles=n_lat_tiles, lat_len=n_lat_tiles * TM)
    return pl.pallas_call(
        kern,
        grid=(bn, N_KV_HEADS, nt),
        in_specs=[
            pl.BlockSpec((1, TM, gw), lambda b, g, t: (b, t, g)),
            pl.BlockSpec((1, tt, HEAD_DIM), lambda b, g, t: (b, 0, g)),
            pl.BlockSpec((1, tt, HEAD_DIM), lambda b, g, t: (b, 0, g)),
        ],
        out_specs=pl.BlockSpec((1, TM, gw), lambda b, g, t: (b, t, g)),
        out_shape=jax.ShapeDtypeStruct((bn, tt, attn_w), BF16),
        compiler_params=_cparams("arbitrary", "arbitrary", "arbitrary"),
        name="attention",
    )(q, k, v)


def _post_mixer(y, x, mod_ref, g_ref):
    xn = x + mod_ref[0, 2:3, :] * _rms(y, g_ref[1:2, :])
    f = _rms(xn, g_ref[2:3, :]) * (1.0 + mod_ref[0, 4:5, :]) + mod_ref[0, 3:4, :]
    return xn, f


def _mix0_kernel(o_ref, pin_ref, pprev_ref, pnext_ref, pw_ref, ps_ref, wout_ref, x_ref, mod_ref, g_ref,
                 xo_ref, f_ref, ext_sc, cat_sc, *, n_lat_tiles, lat_len, ctx_len):
    t = pl.program_id(1)
    is_ctx = t >= n_lat_tiles
    seq_first = jnp.logical_or(t == 0, is_ctx)
    seq_last = jnp.logical_or(t == n_lat_tiles - 1, is_ctx)
    pin = pin_ref[0]
    ext_sc[0:HALO, :] = jnp.where(seq_first, 0.0, pprev_ref[0])
    ext_sc[HALO:HALO + TM, :] = pin
    ext_sc[HALO + TM:, :] = jnp.where(seq_last, 0.0, pnext_ref[0])
    row = lax.broadcasted_iota(jnp.int32, (TM, 1), 0)
    pos = jnp.where(is_ctx, 0, t * TM) + row
    n = jnp.where(is_ctx, ctx_len, lat_len)
    gw = pin.shape[1] // len(POOL_WINDOWS)
    attn_w = o_ref.shape[2]
    cat_sc[:, 0:attn_w] = o_ref[0]
    for gi, w in enumerate(POOL_WINDOWS):
        sl = slice(gi * gw, (gi + 1) * gw)
        base = HALO - w // 2
        acc = ext_sc[base:base + TM, sl]
        for j in range(1, w):
            acc = acc + ext_sc[base + j:base + j + TM, sl]
        cnt = (jnp.minimum(pos + w // 2, n) - jnp.maximum(pos - w // 2, 0)).astype(F32)
        m = acc / cnt - pin[:, sl]
        y = _dot(m.astype(BF16), pw_ref[gi]) * ps_ref[:, sl]
        cat_sc[:, attn_w + gi * gw:attn_w + (gi + 1) * gw] = y.astype(BF16)
    y = _dot(cat_sc[...], wout_ref[...])
    xn, f = _post_mixer(y, x_ref[0], mod_ref, g_ref)
    xo_ref[0] = xn
    f_ref[0] = f


def _mix0_out(o, pin, pool_w_bf, pool_scale, w_out_bf, x_all, mods, g4, n_lat_tiles, lat_len, ctx_len):
    bn, tt, d = x_all.shape
    nt = tt // TM
    attn_w = o.shape[2]
    pw = pin.shape[2]
    hb = TM // HALO
    n_hblk = tt // HALO
    mod_map = lambda b, t: (jnp.where(t < n_lat_tiles, b, bn), 0, 0)
    kern = functools.partial(_mix0_kernel, n_lat_tiles=n_lat_tiles, lat_len=lat_len, ctx_len=ctx_len)
    return pl.pallas_call(
        kern,
        grid=(bn, nt),
        in_specs=[
            pl.BlockSpec((1, TM, attn_w), lambda b, t: (b, t, 0)),
            pl.BlockSpec((1, TM, pw), lambda b, t: (b, t, 0)),
            pl.BlockSpec((1, HALO, pw), lambda b, t: (b, jnp.maximum(t * hb - 1, 0), 0)),
            pl.BlockSpec((1, HALO, pw), lambda b, t: (b, jnp.minimum((t + 1) * hb, n_hblk - 1), 0)),
            pl.BlockSpec(pool_w_bf.shape, lambda b, t: (0, 0, 0)),
            pl.BlockSpec((1, pw), lambda b, t: (0, 0)),
            pl.BlockSpec(w_out_bf.shape, lambda b, t: (0, 0)),
            pl.BlockSpec((1, TM, d), lambda b, t: (b, t, 0)),
            pl.BlockSpec((1, 6, d), mod_map),
            pl.BlockSpec((4, d), lambda b, t: (0, 0)),
        ],
        out_specs=[
            pl.BlockSpec((1, TM, d), lambda b, t: (b, t, 0)),
            pl.BlockSpec((1, TM, d), lambda b, t: (b, t, 0)),
        ],
        out_shape=[jax.ShapeDtypeStruct((bn, tt, d), F32), jax.ShapeDtypeStruct((bn, tt, d), F32)],
        scratch_shapes=[pltpu.VMEM((TM + 2 * HALO, pw), F32), pltpu.VMEM((TM, attn_w + pw), BF16)],
        compiler_params=_cparams("arbitrary", "arbitrary"),
        name="pool_outproj",
    )(o, pin, pin, pin, pool_w_bf, pool_scale.reshape(1, -1), w_out_bf, x_all, mods, g4)


def _router_kernel(f_ref, rw_ref, rb_ref, idx_ref, gate_ref, rank_ref, cnt_ref, carry_sc):
    i = pl.program_id(0)

    @pl.when(i == 0)
    def _():
        carry_sc[...] = jnp.zeros(carry_sc.shape, F32)

    lane = lax.broadcasted_iota(jnp.int32, (TM, LANES), 1).astype(F32)
    logits = _dot(f_ref[...].astype(BF16), rw_ref[...]) + rb_ref[...]
    l = jnp.where(lane < N_EXPERTS, logits, -jnp.inf)
    vals, sels, idxs = [], [], []
    for _ in range(TOP_K):
        m = jnp.max(l, axis=-1, keepdims=True)
        idx = jnp.min(jnp.where(l == m, lane, float(LANES)), axis=-1, keepdims=True)
        sel = lane == idx
        vals.append(m)
        idxs.append(idx)
        sels.append(sel)
        l = jnp.where(sel, -jnp.inf, l)
    es = [jnp.exp(v - vals[0]) for v in vals]
    denom = es[0] + es[1] + es[2] + es[3]
    onehot = jnp.zeros((TM, LANES), F32)
    for sel in sels:
        onehot = onehot + jnp.where(sel, 1.0, 0.0)
    r_i = lax.broadcasted_iota(jnp.int32, (TM, TM), 0)
    c_i = lax.broadcasted_iota(jnp.int32, (TM, TM), 1)
    tril = jnp.where(c_i < r_i, 1.0, 0.0).astype(BF16)
    before = _dot(tril, onehot.astype(BF16)) + carry_sc[0:1, :]
    idx_out = jnp.zeros((TM, LANES), F32)
    gate_out = jnp.zeros((TM, LANES), F32)
    rank_out = jnp.zeros((TM, LANES), F32)
    for k in range(TOP_K):
        rk = jnp.sum(jnp.where(sels[k], before, 0.0), axis=-1, keepdims=True)
        idx_out = jnp.where(lane == k, idxs[k], idx_out)
        gate_out = jnp.where(lane == k, es[k] / denom, gate_out)
        rank_out = jnp.where(lane == k, rk, rank_out)
    idx_ref[...] = idx_out.astype(jnp.int32)
    gate_ref[...] = gate_out
    rank_ref[...] = rank_out.astype(jnp.int32)
    carry_sc[...] = carry_sc[...] + jnp.sum(onehot, axis=0, keepdims=True)
    cnt_ref[...] = carry_sc[...].astype(jnp.int32)


def _router(f2, router_w, router_b):
    n, d = f2.shape
    rw = jnp.zeros((d, LANES), BF16).at[:, :N_EXPERTS].set(router_w.astype(BF16))
    rb = jnp.zeros((1, LANES), F32).at[0, :N_EXPERTS].set(router_b)
    tile = pl.BlockSpec((TM, LANES), lambda i: (i, 0))
    return pl.pallas_call(
        _router_kernel,
        grid=(n // TM,),
        in_specs=[
            pl.BlockSpec((TM, d), lambda i: (i, 0)),
            pl.BlockSpec((d, LANES), lambda i: (0, 0)),
            pl.BlockSpec((1, LANES), lambda i: (0, 0)),
        ],
        out_specs=[tile, tile, tile, pl.BlockSpec((HALO, LANES), lambda i: (0, 0))],
        out_shape=[
            jax.ShapeDtypeStruct((n, LANES), jnp.int32),
            jax.ShapeDtypeStruct((n, LANES), F32),
            jax.ShapeDtypeStruct((n, LANES), jnp.int32),
            jax.ShapeDtypeStruct((HALO, LANES), jnp.int32),
        ],
        scratch_shapes=[pltpu.VMEM((HALO, LANES), F32)],
        compiler_params=_cparams("arbitrary"),
        name="router",
    )(f2, rw, rb)


W_CHUNKS = 4


def _expert_kernel(bs_ref, nxt_ref, tok_ref, f_hbm, wg_hbm, wu_hbm, wd_hbm, bg_ref, bu_ref, bd_ref, y_hbm,
                   wg_bf, wu_bf, wd_bf, stage_a, stage_d, xbuf, ybuf, wsem_a, wsem_d, xsem, ysem,
                   *, layer, n_blocks):
    e = pl.program_id(0)
    n_exp = pl.num_programs(0)
    b0, b1, n_total = bs_ref[e], bs_ref[e + 1], bs_ref[n_exp]
    first_e = nxt_ref[n_exp]
    mats = ((wg_hbm, wg_bf, stage_a, wsem_a), (wu_hbm, wu_bf, stage_a, wsem_a), (wd_hbm, wd_bf, stage_d, wsem_d))
    n_chunks = len(mats) * W_CHUNKS

    def chunk_copy(ee, q):
        src, _, stage, sem = mats[q // W_CHUNKS]
        rows = stage.shape[1]
        c = q % W_CHUNKS
        return pltpu.make_async_copy(src.at[layer, ee, pl.ds(c * rows, rows)], stage.at[q % 2], sem.at[q % 2])

    def gather(blk, slot):
        base = blk * TE

        def body(r, carry):
            tok = tok_ref[base + r]
            pltpu.make_async_copy(f_hbm.at[pl.ds(tok, 1)], xbuf.at[slot, pl.ds(r, 1)], xsem.at[slot]).start()
            return carry

        lax.fori_loop(0, TE, body, 0, unroll=DMA_UNROLL)

    def y_copy(blk):
        return pltpu.make_async_copy(ybuf, y_hbm.at[pl.ds(blk * TE, TE)], ysem.at[0])

    @pl.when(e == first_e)
    def _():
        gather(0, 0)
        chunk_copy(e, 0).start()
        chunk_copy(e, 1).start()

    @pl.when(b1 > b0)
    def _():
        nxt = nxt_ref[e]
        for q in range(n_chunks):
            _, dst, stage, _ = mats[q // W_CHUNKS]
            rows = stage.shape[1]
            c = q % W_CHUNKS
            chunk_copy(e, q).wait()
            dst[c * rows:(c + 1) * rows, :] = stage[q % 2].astype(BF16)
            if q + 2 < n_chunks:
                chunk_copy(e, q + 2).start()
            else:
                @pl.when(nxt < n_exp)
                def _():
                    chunk_copy(nxt, q + 2 - n_chunks).start()

        def block(s, carry):
            slot = s % 2

            @pl.when(s + 1 < n_total)
            def _():
                gather(s + 1, 1 - slot)

            pltpu.make_async_copy(f_hbm.at[pl.ds(0, TE)], xbuf.at[slot], xsem.at[slot]).wait()
            x = xbuf[slot].astype(BF16)
            glu = jnp.minimum(_dot(x, wg_bf[...]) + bg_ref[0], SWIGLU_LIMIT)
            lin = jnp.clip(_dot(x, wu_bf[...]) + bu_ref[0], -SWIGLU_LIMIT, SWIGLU_LIMIT)
            act = glu * _sigmoid(SWIGLU_ALPHA * glu) * (lin + 1.0)
            y = _dot(act.astype(BF16), wd_bf[...]) + bd_ref[0]

            @pl.when(s > 0)
            def _():
                y_copy(s - 1).wait()

            ybuf[...] = y
            y_copy(s).start()
            return carry

        lax.fori_loop(b0, b1, block, 0)

    @pl.when(e == n_exp - 1)
    def _():
        y_copy(n_total - 1).wait()
        ybuf[...] = jnp.zeros(ybuf.shape, F32)

        def fill(s, carry):
            y_copy(s).start()
            y_copy(s).wait()
            return carry

        lax.fori_loop(n_total, n_blocks, fill, 0)


def _experts(f2, blk_start, next_e, slot_tok, layer, wg, bg, wu, bu, wd, bd, n_blocks):
    n, d = f2.shape
    _, ne, _, de = wg.shape
    bias = lambda w: pl.BlockSpec((1, 1, w), lambda e, bs, nx, tok: (e, 0, 0))
    hbm = pl.BlockSpec(memory_space=pl.ANY)
    grid_spec = pltpu.PrefetchScalarGridSpec(
        num_scalar_prefetch=3,
        grid=(ne,),
        in_specs=[hbm, hbm, hbm, hbm, bias(de), bias(de), bias(d)],
        out_specs=hbm,
        scratch_shapes=[
            pltpu.VMEM((d, de), BF16), pltpu.VMEM((d, de), BF16), pltpu.VMEM((de, d), BF16),
            pltpu.VMEM((2, d // W_CHUNKS, de), F32), pltpu.VMEM((2, de // W_CHUNKS, d), F32),
            pltpu.VMEM((2, TE, d), F32), pltpu.VMEM((TE, d), F32),
            pltpu.SemaphoreType.DMA((2,)), pltpu.SemaphoreType.DMA((2,)),
            pltpu.SemaphoreType.DMA((2,)), pltpu.SemaphoreType.DMA((1,)),
        ],
    )
    return pl.pallas_call(
        functools.partial(_expert_kernel, layer=layer, n_blocks=n_blocks),
        grid_spec=grid_spec,
        out_shape=jax.ShapeDtypeStruct((n_blocks * TE, d), F32),
        compiler_params=_cparams("arbitrary"),
        name="experts",
    )(blk_start, next_e, slot_tok, f2, wg, wu, wd, bg.reshape(ne, 1, de), bu.reshape(ne, 1, de), bd.reshape(ne, 1, d))


def _combine_kernel(dest_ref, tmod_ref, yb_hbm, gate_ref, x_ref, mod_ref, g_ref, xo_ref, buf, sem):
    i = pl.program_id(0)
    nt = pl.num_programs(0)

    def issue(tile, slot):
        base = tile * TM * TOP_K

        def body(r, carry):
            for k in range(TOP_K):
                dst = dest_ref[base + r * TOP_K + k]
                pltpu.make_async_copy(yb_hbm.at[pl.ds(dst, 1)], buf.at[slot, k, pl.ds(r, 1)], sem.at[slot]).start()
            return carry

        lax.fori_loop(0, TM, body, 0, unroll=DMA_UNROLL // 2)

    @pl.when(i == 0)
    def _():
        issue(0, 0)

    @pl.when(i + 1 < nt)
    def _():
        issue(i + 1, (i + 1) % 2)

    slot = i % 2
    for k in range(TOP_K):
        pltpu.make_async_copy(yb_hbm.at[pl.ds(0, TM)], buf.at[slot, k], sem.at[slot]).wait()
    gate = gate_ref[...]
    out = gate[:, 0:1] * buf[slot, 0]
    for k in range(1, TOP_K):
        out = out + gate[:, k:k + 1] * buf[slot, k]
    xo_ref[...] = x_ref[...] + mod_ref[0, 5:6, :] * _rms(out, g_ref[3:4, :])


def _combine(yb, dest, tile_mod, gates, x2, mods, g4):
    n, d = x2.shape
    grid_spec = pltpu.PrefetchScalarGridSpec(
        num_scalar_prefetch=2,
        grid=(n // TM,),
        in_specs=[
            pl.BlockSpec(memory_space=pl.ANY),
            pl.BlockSpec((TM, LANES), lambda i, dst, tm: (i, 0)),
            pl.BlockSpec((TM, d), lambda i, dst, tm: (i, 0)),
            pl.BlockSpec((1, 6, d), lambda i, dst, tm: (tm[i], 0, 0)),
            pl.BlockSpec((4, d), lambda i, dst, tm: (0, 0)),
        ],
        out_specs=pl.BlockSpec((TM, d), lambda i, dst, tm: (i, 0)),
        scratch_shapes=[pltpu.VMEM((2, TOP_K, TM, d), F32), pltpu.SemaphoreType.DMA((2,))],
    )
    return pl.pallas_call(
        _combine_kernel,
        grid_spec=grid_spec,
        out_shape=jax.ShapeDtypeStruct((n, d), F32),
        compiler_params=_cparams("arbitrary"),
        name="moe_combine",
    )(dest, tile_mod, yb, gates, x2, mods, g4)


def _moe(f2, x2, tile_mod, mods, g4, router_w, router_b, layer, wg, bg, wu, bu, wd, bd):
    n, d = f2.shape
    idx, gates, rank, cnt = _router(f2, router_w, router_b)
    idx = idx[:, :TOP_K]
    rank = rank[:, :TOP_K]
    counts = cnt[0, :N_EXPERTS]
    padded = (counts + TE - 1) // TE * TE
    pend = jnp.cumsum(padded)
    pstart = pend - padded
    dest = (pstart[idx] + rank).reshape(-1).astype(jnp.int32)
    n_asg = n * TOP_K
    n_blocks = -(-(n_asg + N_EXPERTS * (TE - 1)) // TE)
    blk_start = jnp.concatenate([jnp.zeros((1,), jnp.int32), (pend // TE).astype(jnp.int32)])
    eid = jnp.where(counts > 0, jnp.arange(N_EXPERTS, dtype=jnp.int32), N_EXPERTS)
    first_from = lax.cummin(eid, axis=0, reverse=True)
    next_e = jnp.concatenate([first_from[1:], jnp.full((1,), N_EXPERTS, jnp.int32), first_from[:1]])
    slot_tok = jnp.zeros((n_blocks * TE,), jnp.int32).at[dest].set(jnp.repeat(jnp.arange(n, dtype=jnp.int32), TOP_K))
    yb = _experts(f2, blk_start, next_e, slot_tok, layer, wg, bg, wu, bu, wd, bd, n_blocks)
    return _combine(yb, dest, tile_mod, gates, x2, mods, g4)


def _normmod_kernel(x_ref, mod_ref, g_ref, h_ref):
    h_ref[0] = _rms(x_ref[0], g_ref[0:1, :]) * (1.0 + mod_ref[0, 1:2, :]) + mod_ref[0, 0:1, :]


def _normmod(x_all, mods, g4, n_lat_tiles):
    bn, tt, d = x_all.shape
    mod_map = lambda b, t: (jnp.where(t < n_lat_tiles, b, bn), 0, 0)
    tile = pl.BlockSpec((1, TM, d), lambda b, t: (b, t, 0))
    return pl.pallas_call(
        _normmod_kernel,
        grid=(bn, tt // TM),
        in_specs=[tile, pl.BlockSpec((1, 6, d), mod_map), pl.BlockSpec((4, d), lambda b, t: (0, 0))],
        out_specs=tile,
        out_shape=jax.ShapeDtypeStruct((bn, tt, d), F32),
        compiler_params=_cparams("arbitrary", "arbitrary"),
        name="norm_mod",
    )(x_all, mods, g4)


def _centred_shift(h, hprev_ref, hnext_ref, t, n_lat_tiles, n_tiles):
    seq_first = jnp.logical_or(t == 0, t == n_lat_tiles)
    seq_last = jnp.logical_or(t == n_lat_tiles - 1, t == n_tiles - 1)
    prev_row = jnp.where(seq_first, 0.0, hprev_ref[0, HALO - 1:HALO, :])
    next_row = jnp.where(seq_last, 0.0, hnext_ref[0, 0:1, :])
    tm = h.shape[0]
    row = lax.broadcasted_iota(jnp.int32, (tm, 1), 0)
    prev = jnp.where(row == 0, prev_row, pltpu.roll(h, 1, 0))
    nxt = jnp.where(row == tm - 1, next_row, pltpu.roll(h, tm - 1, 0))
    return 0.5 * (prev + nxt) - h


def _halo_specs(d, tt, tm):
    hb = tm // HALO
    n_hblk = tt // HALO
    return [
        pl.BlockSpec((1, tm, d), lambda *a: (a[-2], a[-1], 0)),
        pl.BlockSpec((1, HALO, d), lambda *a: (a[-2], jnp.maximum(a[-1] * hb - 1, 0), 0)),
        pl.BlockSpec((1, HALO, d), lambda *a: (a[-2], jnp.minimum((a[-1] + 1) * hb, n_hblk - 1), 0)),
    ]


def _shiftproj_kernel(h_ref, hprev_ref, hnext_ref, mix_ref, w_ref, o_ref, *, n_lat_tiles):
    t = pl.program_id(2)
    h = h_ref[0]
    xx = _centred_shift(h, hprev_ref, hnext_ref, t, n_lat_tiles, pl.num_programs(2))
    o_ref[0, 0] =_dot((h + xx * mix_ref[0]).astype(BF16), w_ref[0])


def _shiftproj(h, mix3, w3_bf, n_lat_tiles):
    bn, tt, d = h.shape
    nj = w3_bf.shape[0]
    kern = functools.partial(_shiftproj_kernel, n_lat_tiles=n_lat_tiles)
    return pl.pallas_call(
        kern,
        grid=(nj, bn, tt // TM),
        in_specs=_halo_specs(d, tt, TM) + [
            pl.BlockSpec((1, 1, d), lambda j, b, t: (j, 0, 0)),
            pl.BlockSpec((1, d, d), lambda j, b, t: (j, 0, 0)),
        ],
        out_specs=pl.BlockSpec((1, 1, TM, d), lambda j, b, t: (j, b, t, 0)),
        out_shape=jax.ShapeDtypeStruct((nj, bn, tt, d), F32),
        compiler_params=_cparams("arbitrary", "arbitrary", "arbitrary"),
        name="shift_proj",
    )(h, h, h, mix3, w3_bf)


def _softplus(x):
    return jnp.maximum(x, 0.0) + jnp.log(1.0 + jnp.exp(-jnp.abs(x)))


def _lora_kernel(h_ref, hprev_ref, hnext_ref, mix_ref, w1_ref, a1_ref, g1_ref, w2_ref, a2_ref, g2_ref,
                 w0_ref, a0_ref, dec_ref, aa_ref, gate_ref, *, n_lat_tiles):
    t = pl.program_id(1)
    h = h_ref[0]
    xx = _centred_shift(h, hprev_ref, hnext_ref, t, n_lat_tiles, pl.num_programs(1))
    hw = jnp.tanh(_dot((h + xx * mix_ref[0:1, :]).astype(BF16), w1_ref[...]))
    ha = _dot((h + xx * mix_ref[1:2, :]).astype(BF16), a1_ref[...])
    hg = _sigmoid(_dot((h + xx * mix_ref[2:3, :]).astype(BF16), g1_ref[...]))
    gate_ref[0] = _dot(hg.astype(BF16), g2_ref[...])
    for dd in range(2):
        sl = slice(dd * LORA_PAD, (dd + 1) * LORA_PAD)
        z = w0_ref[dd] + _dot(hw[:, sl].astype(BF16), w2_ref[dd])
        w_log = -_softplus(-z) - 0.5
        dec_ref[dd, 0] = jnp.exp(-jnp.exp(w_log))
        aa_ref[dd, 0] = _sigmoid(a0_ref[dd] + _dot(ha[:, sl].astype(BF16), a2_ref[dd]))


def _pad_lora(w_in, w_out):
    nd, d, r = w_in.shape
    wi = jnp.zeros((d, nd * LORA_PAD), BF16)
    wo = jnp.zeros((nd, LORA_PAD, d), BF16)
    for dd in range(nd):
        wi = wi.at[:, dd * LORA_PAD:dd * LORA_PAD + r].set(w_in[dd].astype(BF16))
        wo = wo.at[dd, :r].set(w_out[dd].astype(BF16))
    return wi, wo


def _lora(h, mix3, w1p, w2p, a1p, a2p, g1, g2, w0, a0, n_lat_tiles):
    bn, tt, d = h.shape
    const2 = lambda s: pl.BlockSpec(s, lambda b, t: (0, 0))
    const3 = lambda s: pl.BlockSpec(s, lambda b, t: (0, 0, 0))
    kern = functools.partial(_lora_kernel, n_lat_tiles=n_lat_tiles * (TM // TL))
    dir_out = pl.BlockSpec((2, 1, TL, d), lambda b, t: (0, b, t, 0))
    tok_out = pl.BlockSpec((1, TL, d), lambda b, t: (b, t, 0))
    dir_shape = jax.ShapeDtypeStruct((2, bn, tt, d), F32)
    tok_shape = jax.ShapeDtypeStruct((bn, tt, d), F32)
    return pl.pallas_call(
        kern,
        grid=(bn, tt // TL),
        in_specs=_halo_specs(d, tt, TL) + [
            const2((3, d)), const2(w1p.shape), const2(a1p.shape), const2((d, g1.shape[1])),
            const3(w2p.shape), const3(a2p.shape), const2(g2.shape),
            const3((2, 1, d)), const3((2, 1, d)),
        ],
        out_specs=[dir_out, dir_out, tok_out],
        out_shape=[dir_shape, dir_shape, tok_shape],
        compiler_params=_cparams("arbitrary", "arbitrary"),
        name="rwkv_lora",
    )(h, h, h, mix3, w1p, a1p, g1, w2p, a2p, g2, w0.reshape(2, 1, d), a0.reshape(2, 1, d))


N_CH = RWKV_HEAD
N_RH = 32
N_GRP = LANES // N_RH


def _colperm_kernel(w_ref, p_ref, o_ref):
    o_ref[...] = _dot(w_ref[...].astype(BF16), p_ref[...]).astype(BF16)


def _colperm(w, p):
    r, d = w.shape
    tr = min(r, 512)
    return pl.pallas_call(
        _colperm_kernel,
        grid=(r // tr,),
        in_specs=[pl.BlockSpec((tr, d), lambda i: (i, 0)), pl.BlockSpec((d, d), lambda i: (0, 0))],
        out_specs=pl.BlockSpec((tr, d), lambda i: (i, 0)),
        out_shape=jax.ShapeDtypeStruct((r, d), BF16),
        compiler_params=_cparams("arbitrary"),
        name="weight_colperm",
    )(w, p)


def _rowperm_kernel(pt_ref, w_ref, o_ref):
    o_ref[...] = _dot(pt_ref[...], w_ref[...].astype(BF16)).astype(BF16)


def _rowperm(w, pt):
    d, n = w.shape
    tn = min(n, 512)
    return pl.pallas_call(
        _rowperm_kernel,
        grid=(n // tn,),
        in_specs=[pl.BlockSpec((d, d), lambda i: (0, 0)), pl.BlockSpec((d, tn), lambda i: (0, i))],
        out_specs=pl.BlockSpec((d, tn), lambda i: (0, i)),
        out_shape=jax.ShapeDtypeStruct((d, n), BF16),
        compiler_params=_cparams("arbitrary"),
        name="weight_rowperm",
    )(pt, w)


def _lane_group(rows):
    return lax.broadcasted_iota(jnp.int32, (rows, LANES), 1) // N_RH


def _pick_groups(pieces, lg):
    out = pieces[N_GRP - 1]
    for g in range(N_GRP - 2, -1, -1):
        out = jnp.where(lg == g, pieces[g], out)
    return out


def _to_chain_blocks(srcs, j, lg):
    blks = [s[:, j * LANES:(j + 1) * LANES] for s in srcs]
    rolled = []
    for sh in range(N_GRP):
        merged = _pick_groups([blks[(i + sh) % N_GRP] for i in range(N_GRP)], lg)
        rolled.append(merged if sh == 0 else pltpu.roll(merged, sh * N_RH, 1))
    return [_pick_groups([rolled[(g - i) % N_GRP] for g in range(N_GRP)], lg) for i in range(N_GRP)]


def _head_sum(x):
    s = x[:, 0:LANES]
    for j in range(1, x.shape[1] // LANES):
        s = s + x[:, j * LANES:(j + 1) * LANES]
    s = s + pltpu.roll(s, 2 * N_RH, 1)
    return s + pltpu.roll(s, N_RH, 1)


def _transpose8(xs):
    sub = lax.broadcasted_iota(jnp.int32, (HALO, LANES), 0)
    xs = list(xs)
    for s in (1, 2, 4):
        low = (sub & s) == 0
        for c0 in range(HALO):
            if c0 & s:
                continue
            a, b = xs[c0], xs[c0 + s]
            xs[c0] = jnp.where(low, a, pltpu.roll(b, s, 0))
            xs[c0 + s] = jnp.where(low, pltpu.roll(a, HALO - s, 0), b)
    return xs


def _store_chain_octet(ref, c0, blks):
    rows = blks[0].shape[0]
    for a in range(rows // HALO):
        tiles = _transpose8([blk[a * HALO:(a + 1) * HALO, :] for blk in blks])
        for t in range(HALO):
            ref[a * HALO + t, c0:c0 + HALO, :] = tiles[t]


def _load_chain_octet(ref, c0):
    rows = ref.shape[0]
    per_a = [_transpose8([ref[a * HALO + t, c0:c0 + HALO, :] for t in range(HALO)]) for a in range(rows // HALO)]
    return [jnp.concatenate([per_a[a][u] for a in range(rows // HALO)], axis=0) for u in range(HALO)]


def _prep_kernel(rkv_ref, dec_ref, aa_ref, kk_ref, ka_ref, rk_ref,
                 w_ref, kd_ref, bb_ref, nk_ref, v_ref, r_ref, bon_ref):
    rows = rkv_ref.shape[2]
    bn = rkv_ref.shape[1]
    lg = _lane_group(rows)
    nblk = kk_ref.shape[1] // LANES
    ks, kks = [], []
    for b in range(bn):
        k = rkv_ref[1, b]
        kk = k * kk_ref[...]
        inv = lax.rsqrt(jnp.maximum(_head_sum(kk * kk), 1e-24))
        ks.append(k)
        kks.append(kk * jnp.concatenate([inv] * nblk, axis=1))
    groups = [(dd, b) for dd in range(2) for b in range(bn)]
    kd_src = [ks[b] * (1.0 + (aa_ref[dd, b] - 1.0) * ka_ref[...]) for dd, b in groups]
    bb_src = [kks[b] * aa_ref[dd, b] for dd, b in groups]
    w_src = [dec_ref[dd, b] for dd, b in groups]
    nk_src = [-kks[b] for _, b in groups]
    v_src = [rkv_ref[2, b] for _, b in groups]
    r_src = [rkv_ref[0, b] for _, b in groups]
    bon = [_head_sum(rkv_ref[0, b] * rk_ref[...] * (kd_src[b] + kd_src[bn + b])) for b in range(bn)]
    bon_ref[...] = _pick_groups([bon[b] for _, b in groups], lg)
    per_oct = HALO // N_GRP
    for c8 in range(N_CH // HALO):
        for srcs, ref in ((w_src, w_ref), (kd_src, kd_ref), (bb_src, bb_ref), (nk_src, nk_ref),
                          (v_src, v_ref), (r_src, r_ref)):
            blks = []
            for j in range(c8 * per_oct, (c8 + 1) * per_oct):
                blks += _to_chain_blocks(srcs, j, lg)
            _store_chain_octet(ref, c8 * HALO, blks)


def _prep(rkv, dec, aa, k_k, k_a, r_k):
    _, bn, tt, d = rkv.shape
    assert bn * 2 == N_GRP and d == N_CH * N_RH
    out = pl.BlockSpec((TP, N_CH, LANES), lambda t: (t, 0, 0))
    shape = jax.ShapeDtypeStruct((tt, N_CH, LANES), F32)
    return pl.pallas_call(
        _prep_kernel,
        grid=(tt // TP,),
        in_specs=[
            pl.BlockSpec((3, bn, TP, d), lambda t: (0, 0, t, 0)),
            pl.BlockSpec((2, bn, TP, d), lambda t: (0, 0, t, 0)),
            pl.BlockSpec((2, bn, TP, d), lambda t: (0, 0, t, 0)),
            pl.BlockSpec((1, d), lambda t: (0, 0)),
            pl.BlockSpec((1, d), lambda t: (0, 0)),
            pl.BlockSpec((1, d), lambda t: (0, 0)),
        ],
        out_specs=[out] * 6 + [pl.BlockSpec((TP, LANES), lambda t: (t, 0))],
        out_shape=[shape] * 6 + [jax.ShapeDtypeStruct((tt, LANES), F32)],
        compiler_params=_cparams("arbitrary"),
        name="rwkv_prep",
    )(rkv, dec, aa, k_k.reshape(1, d), k_a.reshape(1, d), r_k.reshape(1, d))


def _scan_kernel(*refs, steps):
    fwd_refs, rev_refs = refs[0:6], refs[6:12]
    yf_ref, yr_ref, s_sc, op_sc, sa_sc = refs[12:17]
    W, K, B, A, V, R, WR = range(7)
    n_oct = N_CH // HALO
    assert steps % 2 == 0

    @pl.when(pl.program_id(0) == 0)
    def _():
        s_sc[...] = jnp.zeros(s_sc.shape, F32)
        op_sc[...] = jnp.zeros(op_sc.shape, F32)
        sa_sc[...] = jnp.zeros(sa_sc.shape, F32)

    is_fwd = lax.broadcasted_iota(jnp.int32, (1, LANES), 1) < LANES // 2
    octet = lambda j: slice(j * HALO, (j + 1) * HALO)

    def step(i, carry):
        ir = steps - 1 - i
        cur = i % 2
        prv = 1 - cur
        for n, (f, g) in enumerate(zip(fwd_refs, rev_refs)):
            op_sc[cur, n] = jnp.where(is_fwd, f[i], g[ir])
        r = op_sc[cur, R]
        op_sc[cur, WR] = op_sc[cur, W] * r
        br = jnp.sum(op_sc[cur, B] * r, axis=0, keepdims=True)
        kr = jnp.sum(op_sc[cur, K] * r, axis=0, keepdims=True)
        sa_p = [sa_sc[octet(j), :] for j in range(n_oct)]
        vv_p = [op_sc[prv, V, octet(j), :] for j in range(n_oct)]

        def sweep(g, acc):
            sa, y0 = list(acc[:n_oct]), list(acc[n_oct:])
            for u in range(SCAN_UNROLL):
                kk = g * SCAN_UNROLL + u
                w_row = op_sc[prv, W, pl.ds(kk, 1), :]
                b_row = op_sc[prv, B, pl.ds(kk, 1), :]
                k_row = op_sc[prv, K, pl.ds(kk, 1), :]
                a_row = op_sc[cur, A, pl.ds(kk, 1), :]
                wr_row = op_sc[cur, WR, pl.ds(kk, 1), :]
                for j in range(n_oct):
                    s = s_sc[kk, octet(j), :] * w_row + sa_p[j] * b_row + vv_p[j] * k_row
                    s_sc[kk, octet(j), :] = s
                    sa[j] = sa[j] + s * a_row
                    y0[j] = y0[j] + s * wr_row
            return tuple(sa) + tuple(y0)

        zero = jnp.zeros((HALO, LANES), F32)
        acc = lax.fori_loop(0, N_CH // SCAN_UNROLL, sweep, (zero,) * (2 * n_oct))
        for j in range(n_oct):
            sa_j, y0_j = acc[j], acc[n_oct + j]
            sa_sc[octet(j), :] = sa_j
            y = y0_j + sa_j * br + op_sc[cur, V, octet(j), :] * kr
            yf_ref[i, octet(j), :] = y
            yr_ref[ir, octet(j), :] = y
        return carry

    lax.fori_loop(0, steps, step, 0)


def _scan(ops, lat_len, ctx_len, steps=32):
    tt = ops[0].shape[0]
    n_lat, n_ctx = lat_len // steps, ctx_len // steps

    def fwd_blk(i):
        return jnp.where(i < n_ctx, n_lat + i, i - n_ctx)

    def rev_blk(i):
        return jnp.where(i < n_ctx, n_lat + n_ctx - 1 - i, n_lat - 1 - (i - n_ctx))

    fwd = pl.BlockSpec((steps, N_CH, LANES), lambda i: (fwd_blk(i), 0, 0))
    rev = pl.BlockSpec((steps, N_CH, LANES), lambda i: (rev_blk(i), 0, 0))
    shape = jax.ShapeDtypeStruct((tt, N_CH, LANES), F32)
    return pl.pallas_call(
        functools.partial(_scan_kernel, steps=steps),
        grid=(tt // steps,),
        in_specs=[fwd] * 6 + [rev] * 6,
        out_specs=[fwd, rev],
        out_shape=[shape, shape],
        scratch_shapes=[pltpu.VMEM((N_CH, N_CH, LANES), F32), pltpu.VMEM((2, 7, N_CH, LANES), F32),
                        pltpu.VMEM((N_CH, LANES), F32)],
        compiler_params=_cparams("arbitrary"),
        name="rwkv_scan",
    )(*ops, *ops)


def _readout_kernel(yf_ref, yr_ref, v_ref, bon_ref, gate_ref, lnw_ref, lnb_ref, wo_ref,
                    x_ref, mod_ref, g_ref, xo_ref, f_ref, y_sc, o_sc):
    rows = yf_ref.shape[0]
    bn = x_ref.shape[0]
    half = LANES // 2
    lg = _lane_group(rows)
    blk = lambda ref, c: ref[:, c * LANES:(c + 1) * LANES]
    inv_n = 1.0 / N_CH
    mu = None
    for c8 in range(N_CH // HALO):
        yfs = _load_chain_octet(yf_ref, c8 * HALO)
        yrs = _load_chain_octet(yr_ref, c8 * HALO)
        for u in range(HALO):
            y = yfs[u] + pltpu.roll(yrs[u], half, 1)
            y_sc[:, (c8 * HALO + u) * LANES:(c8 * HALO + u + 1) * LANES] = y
            mu = y if mu is None else mu + y
    mu = mu * inv_n
    var = None
    for c in range(N_CH):
        dy = blk(y_sc, c) - mu
        var = dy * dy if var is None else var + dy * dy
    rstd = lax.rsqrt(var * inv_n + GN_EPS)
    bon = bon_ref[...]
    for c8 in range(N_CH // HALO):
        vs = _load_chain_octet(v_ref, c8 * HALO)
        for u in range(HALO):
            c = c8 * HALO + u
            o_sc[:, c * LANES:(c + 1) * LANES] = ((blk(y_sc, c) - mu) * rstd * blk(lnw_ref, c) + blk(lnb_ref, c)
                                                  + bon * vs[u])
    for b in range(bn):
        cols = []
        for j in range(N_CH // N_GRP):
            pieces = []
            for i in range(N_GRP):
                c = j * N_GRP + i
                shift = ((i - b) % N_GRP) * N_RH
                piece = o_sc[:, c * LANES:(c + 1) * LANES]
                pieces.append(piece if shift == 0 else pltpu.roll(piece, shift, 1))
            cols.append(_pick_groups(pieces, lg))
        o = jnp.concatenate(cols, axis=1) * gate_ref[b]
        yo = _dot(o.astype(BF16), wo_ref[...])
        xn = x_ref[b] + mod_ref[b, 2:3, :] * _rms(yo, g_ref[1:2, :])
        xo_ref[b] = xn
        f_ref[b] = _rms(xn, g_ref[2:3, :]) * (1.0 + mod_ref[b, 4:5, :]) + mod_ref[b, 3:4, :]


def _chain_param(p):
    pc = p.reshape(N_RH, N_CH).T
    return jnp.tile(pc, (1, N_GRP)).reshape(1, N_CH * LANES)


def _readout(yf, yr, v_c, bon, gate, ln_w, ln_b, wo_bf, x_all, mods, g4, lat_len):
    bn, tt, d = x_all.shape
    cw = N_CH * LANES
    chain = pl.BlockSpec((TP, N_CH, LANES), lambda t: (t, 0, 0))
    tok = pl.BlockSpec((bn, TP, d), lambda t: (0, t, 0))
    row = pl.BlockSpec((1, cw), lambda t: (0, 0))
    out_shape = jax.ShapeDtypeStruct((bn, lat_len, d), F32)
    return pl.pallas_call(
        _readout_kernel,
        grid=(lat_len // TP,),
        in_specs=[
            chain, chain, chain, pl.BlockSpec((TP, LANES), lambda t: (t, 0)), tok, row, row,
            pl.BlockSpec((d, d), lambda t: (0, 0)),
            tok,
            pl.BlockSpec((bn + 1, 6, d), lambda t: (0, 0, 0)),
            pl.BlockSpec((4, d), lambda t: (0, 0)),
        ],
        out_specs=[tok, tok],
        out_shape=[out_shape, out_shape],
        scratch_shapes=[pltpu.VMEM((TP, cw), F32), pltpu.VMEM((TP, cw), F32)],
        compiler_params=_cparams("arbitrary"),
        name="rwkv_readout",
    )(yf, yr, v_c, bon, gate, _chain_param(ln_w), _chain_param(ln_b), wo_bf, x_all, mods, g4)


def _rope_tables(lat_len, ctx_len):
    m = HEAD_DIM // 4
    t = jnp.arange(lat_len)
    inv = ROPE_THETA ** (-jnp.arange(0, 2 * m, 2, dtype=F32) / (2 * m))
    ar = (t // GRID_W).astype(F32)[:, None] * inv[None, :]
    ac = (t % GRID_W).astype(F32)[:, None] * inv[None, :]
    cos = jnp.concatenate([jnp.cos(ar), jnp.cos(ar), jnp.cos(ac), jnp.cos(ac)], axis=-1)
    sin = jnp.concatenate([-jnp.sin(ar), jnp.sin(ar), -jnp.sin(ac), jnp.sin(ac)], axis=-1)
    cos = jnp.concatenate([cos, jnp.ones((ctx_len, HEAD_DIM), F32)], axis=0)
    sin = jnp.concatenate([sin, jnp.zeros((ctx_len, HEAD_DIM), F32)], axis=0)
    return cos, sin


def kernel(x, c, ctx, c_ctx, w_mod, b_mod, norm_g, ab_w_in, ab_q_gain, ab_k_gain, pool_w, pool_scale, ab_w_out, rw_mix, rw_wr, rw_wk, rw_wv, rw_wo, rw_w0, rw_w1, rw_w2, rw_a0, rw_a1, rw_a2, rw_g1, rw_g2, rw_k_k, rw_k_a, rw_r_k, rw_ln_w, rw_ln_b, router_w, router_b, moe_w_gate, moe_b_gate, moe_w_up, moe_b_up, moe_w_down, moe_b_down):
    bn, lat_len, d = x.shape
    ctx_len = ctx.shape[1]
    assert lat_len % TM == 0 and ctx_len == TM and d % LANES == 0
    n_lat_tiles = lat_len // TM
    tt = lat_len + ctx_len
    nt = tt // TM

    cvec = jnp.zeros((HALO, d), F32).at[:bn].set(c).at[bn].set(c_ctx)
    mods_all = _mod_rows(cvec, w_mod, b_mod)
    mods = [mods_all[i, :bn + 1].reshape(bn + 1, 6, d) for i in range(w_mod.shape[0])]

    x_all = jnp.concatenate([x, ctx], axis=1)

    cos_t, sin_t = _rope_tables(lat_len, ctx_len)
    q, k, v, pin = _inproj(x_all, mods[0], norm_g[0], ab_w_in[0].astype(BF16), ab_q_gain[0], ab_k_gain[0],
                           cos_t, sin_t, n_lat_tiles)
    o = _attention(q, k, v, n_lat_tiles)
    x_all, f = _mix0_out(o, pin, pool_w[0].astype(BF16), pool_scale[0], ab_w_out[0].astype(BF16), x_all,
                         mods[0], norm_g[0], n_lat_tiles, lat_len, ctx_len)
    tile_mod = jnp.where(jnp.arange(bn * nt) % nt < n_lat_tiles, jnp.arange(bn * nt) // nt, bn).astype(jnp.int32)
    x_all = _moe(f.reshape(bn * tt, d), x_all.reshape(bn * tt, d), tile_mod, mods[0], norm_g[0],
                 router_w[0], router_b[0], 0, moe_w_gate, moe_b_gate[0], moe_w_up, moe_b_up[0],
                 moe_w_down, moe_b_down[0]).reshape(bn, tt, d)

    h = _normmod(x_all, mods[1], norm_g[1], n_lat_tiles)
    mix = rw_mix[0]
    perm = jnp.arange(d).reshape(N_RH, N_CH).T.reshape(-1)
    pmat = (jnp.arange(d)[:, None] == perm[None, :]).astype(BF16)
    w3 = _colperm(jnp.concatenate([rw_wr[0], rw_wk[0], rw_wv[0]], axis=0), pmat).reshape(3, d, d)
    rkv = _shiftproj(h, jnp.stack([mix[0], mix[2], mix[3]])[:, None, :], w3, n_lat_tiles)
    w1p, w2p = _pad_lora(rw_w1[0], rw_w2[0])
    a1p, a2p = _pad_lora(rw_a1[0], rw_a2[0])
    w2p = _colperm(w2p.reshape(2 * LORA_PAD, d), pmat).reshape(2, LORA_PAD, d)
    a2p = _colperm(a2p.reshape(2 * LORA_PAD, d), pmat).reshape(2, LORA_PAD, d)
    dec, aa, gate = _lora(h, jnp.stack([mix[1], mix[4], mix[5]]), w1p, w2p, a1p, a2p, rw_g1[0].astype(BF16),
                          _colperm(rw_g2[0], pmat), rw_w0[0][:, perm], rw_a0[0][:, perm], n_lat_tiles)
    w_c, kd_c, bb_c, nk_c, v_c, r_c, bon = _prep(rkv, dec, aa, rw_k_k[0][perm], rw_k_a[0][perm],
                                                 rw_r_k[0].reshape(-1)[perm])
    yf, yr = _scan([w_c, kd_c, bb_c, nk_c, v_c, r_c], lat_len, ctx_len)
    x_lat, f = _readout(yf, yr, v_c, bon, gate, rw_ln_w[0], rw_ln_b[0],
                        _rowperm(rw_wo[0], (perm[:, None] == jnp.arange(d)[None, :]).astype(BF16)), x_all,
                        mods[1], norm_g[1], lat_len)
    tile_mod = (jnp.arange(bn * n_lat_tiles) // n_lat_tiles).astype(jnp.int32)
    out = _moe(f.reshape(bn * lat_len, d), x_lat.reshape(bn * lat_len, d), tile_mod, mods[1], norm_g[1],
               router_w[1], router_b[1], 1, moe_w_gate, moe_b_gate[1], moe_w_up, moe_b_up[1],
               moe_w_down, moe_b_down[1])
    return out.reshape(bn, lat_len, d)
```

```python
import functools

import jax
import jax.numpy as jnp
from jax import lax
from jax.experimental import pallas as pl
from jax.experimental.pallas import tpu as pltpu

F32 = jnp.float32
BF16 = jnp.bfloat16

TM = 256
TL = 128
TP = 64
TE = 256
SCAN_UNROLL = 16
DMA_UNROLL = 8
LANES = 128
HALO = 8
VMEM_LIMIT = 56 * 1024 * 1024

RMS_EPS = 1e-6
GN_EPS = 64e-5
HEAD_DIM = 128
N_HEADS = 8
N_KV_HEADS = 2
GQA_GROUP = N_HEADS // N_KV_HEADS
GRID_W = 64
ROPE_THETA = 10000.0
POOL_WINDOWS = (2, 4, 8, 16)
RWKV_HEAD = 64
LORA_PAD = 128
N_EXPERTS = 32
TOP_K = 4
SWIGLU_LIMIT = 7.0
SWIGLU_ALPHA = 1.702


def _cparams(*sem):
    return pltpu.CompilerParams(dimension_semantics=sem, vmem_limit_bytes=VMEM_LIMIT)


def _rms(x, g):
    return x * lax.rsqrt(jnp.mean(x * x, axis=-1, keepdims=True) + RMS_EPS) * g


def _sigmoid(x):
    return 1.0 / (1.0 + jnp.exp(-x))


def _dot(a, b):
    return jnp.dot(a, b, preferred_element_type=F32)


def _mod_kernel(c_ref, w_ref, b_ref, o_ref):
    c = c_ref[...]
    s = c * _sigmoid(c)
    o_ref[0] = _dot(s.astype(BF16), w_ref[0].astype(BF16)) + b_ref[0]


def _mod_rows(cvec, w_mod, b_mod):
    depth, d, n = w_mod.shape
    tn = 1024
    rows = cvec.shape[0]
    return pl.pallas_call(
        _mod_kernel,
        grid=(depth, n // tn),
        in_specs=[
            pl.BlockSpec((rows, d), lambda i, j: (0, 0)),
            pl.BlockSpec((1, d, tn), lambda i, j: (i, 0, j)),
            pl.BlockSpec((1, 1, tn), lambda i, j: (i, 0, j)),
        ],
        out_specs=pl.BlockSpec((1, rows, tn), lambda i, j: (i, 0, j)),
        out_shape=jax.ShapeDtypeStruct((depth, rows, n), F32),
        compiler_params=_cparams("arbitrary", "arbitrary"),
        name="mod_rows",
    )(cvec, w_mod, b_mod.reshape(depth, 1, n))


def _inproj_kernel(x_ref, mod_ref, g_ref, w_ref, qg_ref, kg_ref, cos_ref, sin_ref,
                   q_ref, k_ref, v_ref, pin_ref):
    x = x_ref[0]
    h = _rms(x, g_ref[0:1, :]) * (1.0 + mod_ref[0, 1:2, :]) + mod_ref[0, 0:1, :]
    u = _dot(h.astype(BF16), w_ref[...])
    cos = cos_ref[...]
    sin = sin_ref[...]
    lane = lax.broadcasted_iota(jnp.int32, (TM, HEAD_DIM), 1)
    first_half = (lane & 32) == 0

    def head(xh, gain, scale):
        y = _rms(xh, gain)
        sw = jnp.where(first_half, pltpu.roll(y, 96, 1), pltpu.roll(y, 32, 1))
        return ((y * cos + sw * sin) * scale).astype(BF16)

    attn_w = N_HEADS * HEAD_DIM
    kv_w = N_KV_HEADS * HEAD_DIM
    for hh in range(N_HEADS):
        sl = slice(hh * HEAD_DIM, (hh + 1) * HEAD_DIM)
        q_ref[0, :, sl] = head(u[:, sl], qg_ref[...], HEAD_DIM ** -0.5)
    for hh in range(N_KV_HEADS):
        sl = slice(hh * HEAD_DIM, (hh + 1) * HEAD_DIM)
        k_ref[0, :, sl] = head(u[:, attn_w + hh * HEAD_DIM:attn_w + (hh + 1) * HEAD_DIM], kg_ref[...], 1.0)
    v_ref[0] = u[:, attn_w + kv_w:attn_w + 2 * kv_w].astype(BF16)
    pin_ref[0] = u[:, attn_w + 2 * kv_w:]


def _inproj(x_all, mods, g4, w_in_bf, q_gain, k_gain, cos_t, sin_t, n_lat_tiles):
    bn, tt, d = x_all.shape
    nt = tt // TM
    attn_w = N_HEADS * HEAD_DIM
    kv_w = N_KV_HEADS * HEAD_DIM
    pool_w = w_in_bf.shape[1] - attn_w - 2 * kv_w
    mod_map = lambda b, t: (jnp.where(t < n_lat_tiles, b, bn), 0, 0)
    return pl.pallas_call(
        _inproj_kernel,
        grid=(bn, nt),
        in_specs=[
            pl.BlockSpec((1, TM, d), lambda b, t: (b, t, 0)),
            pl.BlockSpec((1, 6, d), mod_map),
            pl.BlockSpec((4, d), lambda b, t: (0, 0)),
            pl.BlockSpec(w_in_bf.shape, lambda b, t: (0, 0)),
            pl.BlockSpec((1, HEAD_DIM), lambda b, t: (0, 0)),
            pl.BlockSpec((1, HEAD_DIM), lambda b, t: (0, 0)),
            pl.BlockSpec((TM, HEAD_DIM), lambda b, t: (t, 0)),
            pl.BlockSpec((TM, HEAD_DIM), lambda b, t: (t, 0)),
        ],
        out_specs=[
            pl.BlockSpec((1, TM, attn_w), lambda b, t: (b, t, 0)),
            pl.BlockSpec((1, TM, kv_w), lambda b, t: (b, t, 0)),
            pl.BlockSpec((1, TM, kv_w), lambda b, t: (b, t, 0)),
            pl.BlockSpec((1, TM, pool_w), lambda b, t: (b, t, 0)),
        ],
        out_shape=[
            jax.ShapeDtypeStruct((bn, tt, attn_w), BF16),
            jax.ShapeDtypeStruct((bn, tt, kv_w), BF16),
            jax.ShapeDtypeStruct((bn, tt, kv_w), BF16),
            jax.ShapeDtypeStruct((bn, tt, pool_w), F32),
        ],
        compiler_params=_cparams("arbitrary", "arbitrary"),
        name="inproj",
    )(x_all, mods, g4, w_in_bf, q_gain.reshape(1, -1), k_gain.reshape(1, -1), cos_t, sin_t)


def _attn_kernel(q_ref, k_ref, v_ref, o_ref, *, n_lat_tiles, lat_len):
    t = pl.program_id(2)

    def attend(kk, vv):
        for i in range(GQA_GROUP):
            sl = slice(i * HEAD_DIM, (i + 1) * HEAD_DIM)
            s = lax.dot_general(q_ref[0, :, sl], kk, (((1,), (1,)), ((), ())), preferred_element_type=F32)
            p = jnp.exp(s - jnp.max(s, axis=-1, keepdims=True))
            l = jnp.sum(p, axis=-1, keepdims=True)
            o_ref[0, :, sl] = (_dot(p.astype(BF16), vv) / l).astype(BF16)

    @pl.when(t < n_lat_tiles)
    def _():
        attend(k_ref[0], v_ref[0])

    @pl.when(t >= n_lat_tiles)
    def _():
        attend(k_ref[0, lat_len:, :], v_ref[0, lat_len:, :])


def _attention(q, k, v, n_lat_tiles):
    bn, tt, attn_w = q.shape
    nt = tt // TM
    gw = GQA_GROUP * HEAD_DIM
    kern = functools.partial(_attn_kernel, n_lat_tiles=n_lat_tiles, lat_len=n_lat_tiles * TM)
    return pl.pallas_call(
        kern,
        grid=(bn, N_KV_HEADS, nt),
        in_specs=[
            pl.BlockSpec((1, TM, gw), lambda b, g, t: (b, t, g)),
            pl.BlockSpec((1, tt, HEAD_DIM), lambda b, g, t: (b, 0, g)),
            pl.BlockSpec((1, tt, HEAD_DIM), lambda b, g, t: (b, 0, g)),
        ],
        out_specs=pl.BlockSpec((1, TM, gw), lambda b, g, t: (b, t, g)),
        out_shape=jax.ShapeDtypeStruct((bn, tt, attn_w), BF16),
        compiler_params=_cparams("arbitrary", "arbitrary", "arbitrary"),
        name="attention",
    )(q, k, v)


def _post_mixer(y, x, mod_ref, g_ref):
    xn = x + mod_ref[0, 2:3, :] * _rms(y, g_ref[1:2, :])
    f = _rms(xn, g_ref[2:3, :]) * (1.0 + mod_ref[0, 4:5, :]) + mod_ref[0, 3:4, :]
    return xn, f


def _mix0_kernel(o_ref, pin_ref, pprev_ref, pnext_ref, pw_ref, ps_ref, wout_ref, x_ref, mod_ref, g_ref,
                 xo_ref, f_ref, ext_sc, cat_sc, *, n_lat_tiles, lat_len, ctx_len):
    t = pl.program_id(1)
    is_ctx = t >= n_lat_tiles
    seq_first = jnp.logical_or(t == 0, is_ctx)
    seq_last = jnp.logical_or(t == n_lat_tiles - 1, is_ctx)
    pin = pin_ref[0]
    ext_sc[0:HALO, :] = jnp.where(seq_first, 0.0, pprev_ref[0])
    ext_sc[HALO:HALO + TM, :] = pin
    ext_sc[HALO + TM:, :] = jnp.where(seq_last, 0.0, pnext_ref[0])
    row = lax.broadcasted_iota(jnp.int32, (TM, 1), 0)
    pos = jnp.where(is_ctx, 0, t * TM) + row
    n = jnp.where(is_ctx, ctx_len, lat_len)
    gw = pin.shape[1] // len(POOL_WINDOWS)
    attn_w = o_ref.shape[2]
    cat_sc[:, 0:attn_w] = o_ref[0]
    for gi, w in enumerate(POOL_WINDOWS):
        sl = slice(gi * gw, (gi + 1) * gw)
        base = HALO - w // 2
        acc = ext_sc[base:base + TM, sl]
        for j in range(1, w):
            acc = acc + ext_sc[base + j:base + j + TM, sl]
        cnt = (jnp.minimum(pos + w // 2, n) - jnp.maximum(pos - w // 2, 0)).astype(F32)
        m = acc / cnt - pin[:, sl]
        y = _dot(m.astype(BF16), pw_ref[gi]) * ps_ref[:, sl]
        cat_sc[:, attn_w + gi * gw:attn_w + (gi + 1) * gw] = y.astype(BF16)
    y = _dot(cat_sc[...], wout_ref[...])
    xn, f = _post_mixer(y, x_ref[0], mod_ref, g_ref)
    xo_ref[0] = xn
    f_ref[0] = f


def _mix0_out(o, pin, pool_w_bf, pool_scale, w_out_bf, x_all, mods, g4, n_lat_tiles, lat_len, ctx_len):
    bn, tt, d = x_all.shape
    nt = tt // TM
    attn_w = o.shape[2]
    pw = pin.shape[2]
    hb = TM // HALO
    n_hblk = tt // HALO
    mod_map = lambda b, t: (jnp.where(t < n_lat_tiles, b, bn), 0, 0)
    kern = functools.partial(_mix0_kernel, n_lat_tiles=n_lat_tiles, lat_len=lat_len, ctx_len=ctx_len)
    return pl.pallas_call(
        kern,
        grid=(bn, nt),
        in_specs=[
            pl.BlockSpec((1, TM, attn_w), lambda b, t: (b, t, 0)),
            pl.BlockSpec((1, TM, pw), lambda b, t: (b, t, 0)),
            pl.BlockSpec((1, HALO, pw), lambda b, t: (b, jnp.maximum(t * hb - 1, 0), 0)),
            pl.BlockSpec((1, HALO, pw), lambda b, t: (b, jnp.minimum((t + 1) * hb, n_hblk - 1), 0)),
            pl.BlockSpec(pool_w_bf.shape, lambda b, t: (0, 0, 0)),
            pl.BlockSpec((1, pw), lambda b, t: (0, 0)),
            pl.BlockSpec(w_out_bf.shape, lambda b, t: (0, 0)),
            pl.BlockSpec((1, TM, d), lambda b, t: (b, t, 0)),
            pl.BlockSpec((1, 6, d), mod_map),
            pl.BlockSpec((4, d), lambda b, t: (0, 0)),
        ],
        out_specs=[
            pl.BlockSpec((1, TM, d), lambda b, t: (b, t, 0)),
            pl.BlockSpec((1, TM, d), lambda b, t: (b, t, 0)),
        ],
        out_shape=[jax.ShapeDtypeStruct((bn, tt, d), F32), jax.ShapeDtypeStruct((bn, tt, d), F32)],
        scratch_shapes=[pltpu.VMEM((TM + 2 * HALO, pw), F32), pltpu.VMEM((TM, attn_w + pw), BF16)],
        compiler_params=_cparams("arbitrary", "arbitrary"),
        name="pool_outproj",
    )(o, pin, pin, pin, pool_w_bf, pool_scale.reshape(1, -1), w_out_bf, x_all, mods, g4)


def _router_kernel(f_ref, rw_ref, rb_ref, idx_ref, gate_ref, rank_ref, cnt_ref, carry_sc):
    i = pl.program_id(0)

    @pl.when(i == 0)
    def _():
        carry_sc[...] = jnp.zeros(carry_sc.shape, F32)

    lane = lax.broadcasted_iota(jnp.int32, (TM, LANES), 1).astype(F32)
    logits = _dot(f_ref[...].astype(BF16), rw_ref[...]) + rb_ref[...]
    l = jnp.where(lane < N_EXPERTS, logits, -jnp.inf)
    vals, sels, idxs = [], [], []
    for _ in range(TOP_K):
        m = jnp.max(l, axis=-1, keepdims=True)
        idx = jnp.min(jnp.where(l == m, lane, float(LANES)), axis=-1, keepdims=True)
        sel = lane == idx
        vals.append(m)
        idxs.append(idx)
        sels.append(sel)
        l = jnp.where(sel, -jnp.inf, l)
    es = [jnp.exp(v - vals[0]) for v in vals]
    denom = es[0] + es[1] + es[2] + es[3]
    onehot = jnp.zeros((TM, LANES), F32)
    for sel in sels:
        onehot = onehot + jnp.where(sel, 1.0, 0.0)
    r_i = lax.broadcasted_iota(jnp.int32, (TM, TM), 0)
    c_i = lax.broadcasted_iota(jnp.int32, (TM, TM), 1)
    tril = jnp.where(c_i < r_i, 1.0, 0.0).astype(BF16)
    before = _dot(tril, onehot.astype(BF16)) + carry_sc[0:1, :]
    idx_out = jnp.zeros((TM, LANES), F32)
    gate_out = jnp.zeros((TM, LANES), F32)
    rank_out = jnp.zeros((TM, LANES), F32)
    for k in range(TOP_K):
        rk = jnp.sum(jnp.where(sels[k], before, 0.0), axis=-1, keepdims=True)
        idx_out = jnp.where(lane == k, idxs[k], idx_out)
        gate_out = jnp.where(lane == k, es[k] / denom, gate_out)
        rank_out = jnp.where(lane == k, rk, rank_out)
    idx_ref[...] = idx_out.astype(jnp.int32)
    gate_ref[...] = gate_out
    rank_ref[...] = rank_out.astype(jnp.int32)
    carry_sc[...] = carry_sc[...] + jnp.sum(onehot, axis=0, keepdims=True)
    cnt_ref[...] = carry_sc[...].astype(jnp.int32)


def _router(f2, router_w, router_b):
    n, d = f2.shape
    rw = jnp.zeros((d, LANES), BF16).at[:, :N_EXPERTS].set(router_w.astype(BF16))
    rb = jnp.zeros((1, LANES), F32).at[0, :N_EXPERTS].set(router_b)
    tile = pl.BlockSpec((TM, LANES), lambda i: (i, 0))
    return pl.pallas_call(
        _router_kernel,
        grid=(n // TM,),
        in_specs=[
            pl.BlockSpec((TM, d), lambda i: (i, 0)),
            pl.BlockSpec((d, LANES), lambda i: (0, 0)),
            pl.BlockSpec((1, LANES), lambda i: (0, 0)),
        ],
        out_specs=[tile, tile, tile, pl.BlockSpec((HALO, LANES), lambda i: (0, 0))],
        out_shape=[
            jax.ShapeDtypeStruct((n, LANES), jnp.int32),
            jax.ShapeDtypeStruct((n, LANES), F32),
            jax.ShapeDtypeStruct((n, LANES), jnp.int32),
            jax.ShapeDtypeStruct((HALO, LANES), jnp.int32),
        ],
        scratch_shapes=[pltpu.VMEM((HALO, LANES), F32)],
        compiler_params=_cparams("arbitrary"),
        name="router",
    )(f2, rw, rb)


W_CHUNKS = 4


def _expert_kernel(bs_ref, nxt_ref, par_ref, cpb_ref, tok_ref, f_hbm, wg_hbm, wu_hbm, wd_hbm, bg_ref, bu_ref, bd_ref,
                   y_hbm, wg_bf, wu_bf, wd_bf, stage_a, stage_d, xbuf, ybuf, wsem_a, wsem_d, xsem, ysem,
                   *, layer, n_blocks):
    e = pl.program_id(0)
    n_exp = pl.num_programs(0)
    b0, b1, n_total = bs_ref[e], bs_ref[e + 1], bs_ref[n_exp]
    first_e = nxt_ref[n_exp]
    mats = ((wg_hbm, wg_bf, stage_a, wsem_a), (wu_hbm, wu_bf, stage_a, wsem_a), (wd_hbm, wd_bf, stage_d, wsem_d))
    n_chunks = len(mats) * W_CHUNKS

    def chunk_copy(ee, kind, c, slot):
        src, _, stage, sem = mats[kind]
        rows = stage.shape[1]
        off = pl.multiple_of(c * rows, rows)
        return pltpu.make_async_copy(src.at[layer, ee, pl.ds(off, rows)], stage.at[slot], sem.at[slot])

    def by_kind(q, fn):
        for kind in range(len(mats)):
            @pl.when(q // W_CHUNKS == kind)
            def _():
                fn(kind, q % W_CHUNKS)

    def start_chunk(ee, q):
        by_kind(q, lambda kind, c: chunk_copy(ee, kind, c, q % 2).start())

    def process_chunk(ee, q, tslot):
        def cast(kind, c):
            _, dst, stage, _ = mats[kind]
            rows = stage.shape[1]
            chunk_copy(ee, kind, c, q % 2).wait()
            dst[tslot, pl.ds(pl.multiple_of(c * rows, rows), rows), :] = stage[q % 2].astype(BF16)

        by_kind(q, cast)

        @pl.when(q + 2 < n_chunks)
        def _():
            start_chunk(ee, q + 2)

    def process_chunks(ee, lo, hi, tslot):
        def body(q, carry):
            process_chunk(ee, q, tslot)
            return carry

        lax.fori_loop(lo, hi, body, 0)

    def gather(blk, slot):
        base = blk * TE

        def body(r, carry):
            tok = tok_ref[base + r]
            pltpu.make_async_copy(f_hbm.at[pl.ds(tok, 1)], xbuf.at[slot, pl.ds(r, 1)], xsem.at[slot]).start()
            return carry

        lax.fori_loop(0, TE, body, 0, unroll=DMA_UNROLL)

    def y_copy(blk):
        return pltpu.make_async_copy(ybuf, y_hbm.at[pl.ds(blk * TE, TE)], ysem.at[0])

    zero = jnp.int32(0)

    @pl.when(e == first_e)
    def _():
        gather(0, 0)
        start_chunk(e, zero)
        start_chunk(e, zero + 1)
        process_chunks(e, 0, n_chunks, par_ref[e])

    @pl.when(b1 > b0)
    def _():
        wslot = par_ref[e]
        nxt = nxt_ref[e]
        has_next = nxt < n_exp
        cpb = cpb_ref[e]

        @pl.when(has_next)
        def _():
            start_chunk(nxt, zero)
            start_chunk(nxt, zero + 1)

        def block(s, carry):
            slot = s % 2

            @pl.when(s + 1 < n_total)
            def _():
                gather(s + 1, 1 - slot)

            pltpu.make_async_copy(f_hbm.at[pl.ds(0, TE)], xbuf.at[slot], xsem.at[slot]).wait()
            x = xbuf[slot].astype(BF16)
            glu = jnp.minimum(_dot(x, wg_bf[wslot]) + bg_ref[0], SWIGLU_LIMIT)
            lin = jnp.clip(_dot(x, wu_bf[wslot]) + bu_ref[0], -SWIGLU_LIMIT, SWIGLU_LIMIT)
            act = glu * _sigmoid(SWIGLU_ALPHA * glu) * (lin + 1.0)
            y = _dot(act.astype(BF16), wd_bf[wslot]) + bd_ref[0]

            @pl.when(s > 0)
            def _():
                y_copy(s - 1).wait()

            ybuf[...] = y
            y_copy(s).start()

            @pl.when(has_next)
            def _():
                lo = (s - b0) * cpb
                process_chunks(nxt, lo, jnp.minimum(lo + cpb, n_chunks), 1 - wslot)

            return carry

        lax.fori_loop(b0, b1, block, 0)

    @pl.when(e == n_exp - 1)
    def _():
        y_copy(n_total - 1).wait()
        ybuf[...] = jnp.zeros(ybuf.shape, F32)

        def fill(s, carry):
            y_copy(s).start()
            y_copy(s).wait()
            return carry

        lax.fori_loop(n_total, n_blocks, fill, 0)


def _experts(f2, blk_start, next_e, wslot, cpb, slot_tok, layer, wg, bg, wu, bu, wd, bd, n_blocks):
    n, d = f2.shape
    _, ne, _, de = wg.shape
    bias = lambda w: pl.BlockSpec((1, 1, w), lambda e, *_: (e, 0, 0))
    hbm = pl.BlockSpec(memory_space=pl.ANY)
    grid_spec = pltpu.PrefetchScalarGridSpec(
        num_scalar_prefetch=5,
        grid=(ne,),
        in_specs=[hbm, hbm, hbm, hbm, bias(de), bias(de), bias(d)],
        out_specs=hbm,
        scratch_shapes=[
            pltpu.VMEM((2, d, de), BF16), pltpu.VMEM((2, d, de), BF16), pltpu.VMEM((2, de, d), BF16),
            pltpu.VMEM((2, d // W_CHUNKS, de), F32), pltpu.VMEM((2, de // W_CHUNKS, d), F32),
            pltpu.VMEM((2, TE, d), F32), pltpu.VMEM((TE, d), F32),
            pltpu.SemaphoreType.DMA((2,)), pltpu.SemaphoreType.DMA((2,)),
            pltpu.SemaphoreType.DMA((2,)), pltpu.SemaphoreType.DMA((1,)),
        ],
    )
    return pl.pallas_call(
        functools.partial(_expert_kernel, layer=layer, n_blocks=n_blocks),
        grid_spec=grid_spec,
        out_shape=jax.ShapeDtypeStruct((n_blocks * TE, d), F32),
        compiler_params=_cparams("arbitrary"),
        name="experts",
    )(blk_start, next_e, wslot, cpb, slot_tok, f2, wg, wu, wd,
      bg.reshape(ne, 1, de), bu.reshape(ne, 1, de), bd.reshape(ne, 1, d))


def _combine_kernel(dest_ref, tmod_ref, yb_hbm, gate_ref, x_ref, mod_ref, g_ref, xo_ref, buf, sem):
    i = pl.program_id(0)
    nt = pl.num_programs(0)

    def issue(tile, slot):
        base = tile * TM * TOP_K

        def body(r, carry):
            for k in range(TOP_K):
                dst = dest_ref[base + r * TOP_K + k]
                pltpu.make_async_copy(yb_hbm.at[pl.ds(dst, 1)], buf.at[slot, k, pl.ds(r, 1)], sem.at[slot]).start()
            return carry

        lax.fori_loop(0, TM, body, 0, unroll=DMA_UNROLL // 2)

    @pl.when(i == 0)
    def _():
        issue(0, 0)

    @pl.when(i + 1 < nt)
    def _():
        issue(i + 1, (i + 1) % 2)

    slot = i % 2
    for k in range(TOP_K):
        pltpu.make_async_copy(yb_hbm.at[pl.ds(0, TM)], buf.at[slot, k], sem.at[slot]).wait()
    gate = gate_ref[...]
    out = gate[:, 0:1] * buf[slot, 0]
    for k in range(1, TOP_K):
        out = out + gate[:, k:k + 1] * buf[slot, k]
    xo_ref[...] = x_ref[...] + mod_ref[0, 5:6, :] * _rms(out, g_ref[3:4, :])


def _combine(yb, dest, tile_mod, gates, x2, mods, g4):
    n, d = x2.shape
    grid_spec = pltpu.PrefetchScalarGridSpec(
        num_scalar_prefetch=2,
        grid=(n // TM,),
        in_specs=[
            pl.BlockSpec(memory_space=pl.ANY),
            pl.BlockSpec((TM, LANES), lambda i, dst, tm: (i, 0)),
            pl.BlockSpec((TM, d), lambda i, dst, tm: (i, 0)),
            pl.BlockSpec((1, 6, d), lambda i, dst, tm: (tm[i], 0, 0)),
            pl.BlockSpec((4, d), lambda i, dst, tm: (0, 0)),
        ],
        out_specs=pl.BlockSpec((TM, d), lambda i, dst, tm: (i, 0)),
        scratch_shapes=[pltpu.VMEM((2, TOP_K, TM, d), F32), pltpu.SemaphoreType.DMA((2,))],
    )
    return pl.pallas_call(
        _combine_kernel,
        grid_spec=grid_spec,
        out_shape=jax.ShapeDtypeStruct((n, d), F32),
        compiler_params=_cparams("arbitrary"),
        name="moe_combine",
    )(dest, tile_mod, yb, gates, x2, mods, g4)


def _moe(f2, x2, tile_mod, mods, g4, router_w, router_b, layer, wg, bg, wu, bu, wd, bd):
    n, d = f2.shape
    idx, gates, rank, cnt = _router(f2, router_w, router_b)
    idx = idx[:, :TOP_K]
    rank = rank[:, :TOP_K]
    counts = cnt[0, :N_EXPERTS]
    padded = (counts + TE - 1) // TE * TE
    pend = jnp.cumsum(padded)
    pstart = pend - padded
    dest = (pstart[idx] + rank).reshape(-1).astype(jnp.int32)
    n_asg = n * TOP_K
    n_blocks = -(-(n_asg + N_EXPERTS * (TE - 1)) // TE)
    blk_start = jnp.concatenate([jnp.zeros((1,), jnp.int32), (pend // TE).astype(jnp.int32)])
    eid = jnp.where(counts > 0, jnp.arange(N_EXPERTS, dtype=jnp.int32), N_EXPERTS)
    first_from = lax.cummin(eid, axis=0, reverse=True)
    next_e = jnp.concatenate([first_from[1:], jnp.full((1,), N_EXPERTS, jnp.int32), first_from[:1]])
    slot_tok = jnp.zeros((n_blocks * TE,), jnp.int32).at[dest].set(jnp.repeat(jnp.arange(n, dtype=jnp.int32), TOP_K))
    has = counts > 0
    wslot = jnp.where(has, (jnp.cumsum(has.astype(jnp.int32)) - 1) % 2, 0).astype(jnp.int32)
    nblk = (padded // TE).astype(jnp.int32)
    n_wchunks = 3 * W_CHUNKS
    cpb = jnp.where(has, (n_wchunks + nblk - 1) // jnp.maximum(nblk, 1), 0).astype(jnp.int32)
    yb = _experts(f2, blk_start, next_e, wslot, cpb, slot_tok, layer, wg, bg, wu, bu, wd, bd, n_blocks)
    return _combine(yb, dest, tile_mod, gates, x2, mods, g4)


def _normmod_kernel(x_ref, mod_ref, g_ref, h_ref):
    h_ref[0] = _rms(x_ref[0], g_ref[0:1, :]) * (1.0 + mod_ref[0, 1:2, :]) + mod_ref[0, 0:1, :]


def _normmod(x_all, mods, g4, n_lat_tiles):
    bn, tt, d = x_all.shape
    mod_map = lambda b, t: (jnp.where(t < n_lat_tiles, b, bn), 0, 0)
    tile = pl.BlockSpec((1, TM, d), lambda b, t: (b, t, 0))
    return pl.pallas_call(
        _normmod_kernel,
        grid=(bn, tt // TM),
        in_specs=[tile, pl.BlockSpec((1, 6, d), mod_map), pl.BlockSpec((4, d), lambda b, t: (0, 0))],
        out_specs=tile,
        out_shape=jax.ShapeDtypeStruct((bn, tt, d), F32),
        compiler_params=_cparams("arbitrary", "arbitrary"),
        name="norm_mod",
    )(x_all, mods, g4)


def _centred_shift(h, hprev_ref, hnext_ref, t, n_lat_tiles, n_tiles):
    seq_first = jnp.logical_or(t == 0, t == n_lat_tiles)
    seq_last = jnp.logical_or(t == n_lat_tiles - 1, t == n_tiles - 1)
    prev_row = jnp.where(seq_first, 0.0, hprev_ref[0, HALO - 1:HALO, :])
    next_row = jnp.where(seq_last, 0.0, hnext_ref[0, 0:1, :])
    tm = h.shape[0]
    row = lax.broadcasted_iota(jnp.int32, (tm, 1), 0)
    prev = jnp.where(row == 0, prev_row, pltpu.roll(h, 1, 0))
    nxt = jnp.where(row == tm - 1, next_row, pltpu.roll(h, tm - 1, 0))
    return 0.5 * (prev + nxt) - h


def _halo_specs(d, tt, tm):
    hb = tm // HALO
    n_hblk = tt // HALO
    return [
        pl.BlockSpec((1, tm, d), lambda *a: (a[-2], a[-1], 0)),
        pl.BlockSpec((1, HALO, d), lambda *a: (a[-2], jnp.maximum(a[-1] * hb - 1, 0), 0)),
        pl.BlockSpec((1, HALO, d), lambda *a: (a[-2], jnp.minimum((a[-1] + 1) * hb, n_hblk - 1), 0)),
    ]


def _shiftproj_kernel(h_ref, hprev_ref, hnext_ref, mix_ref, w_ref, o_ref, *, n_lat_tiles):
    t = pl.program_id(2)
    h = h_ref[0]
    xx = _centred_shift(h, hprev_ref, hnext_ref, t, n_lat_tiles, pl.num_programs(2))
    o_ref[0, 0] =_dot((h + xx * mix_ref[0]).astype(BF16), w_ref[0])


def _shiftproj(h, mix3, w3_bf, n_lat_tiles):
    bn, tt, d = h.shape
    nj = w3_bf.shape[0]
    kern = functools.partial(_shiftproj_kernel, n_lat_tiles=n_lat_tiles)
    return pl.pallas_call(
        kern,
        grid=(nj, bn, tt // TM),
        in_specs=_halo_specs(d, tt, TM) + [
            pl.BlockSpec((1, 1, d), lambda j, b, t: (j, 0, 0)),
            pl.BlockSpec((1, d, d), lambda j, b, t: (j, 0, 0)),
        ],
        out_specs=pl.BlockSpec((1, 1, TM, d), lambda j, b, t: (j, b, t, 0)),
        out_shape=jax.ShapeDtypeStruct((nj, bn, tt, d), F32),
        compiler_params=_cparams("arbitrary", "arbitrary", "arbitrary"),
        name="shift_proj",
    )(h, h, h, mix3, w3_bf)


def _softplus(x):
    return jnp.maximum(x, 0.0) + jnp.log(1.0 + jnp.exp(-jnp.abs(x)))


def _lora_kernel(h_ref, hprev_ref, hnext_ref, mix_ref, w1_ref, a1_ref, g1_ref, w2_ref, a2_ref, g2_ref,
                 w0_ref, a0_ref, dec_ref, aa_ref, gate_ref, *, n_lat_tiles):
    t = pl.program_id(1)
    h = h_ref[0]
    xx = _centred_shift(h, hprev_ref, hnext_ref, t, n_lat_tiles, pl.num_programs(1))
    hw = jnp.tanh(_dot((h + xx * mix_ref[0:1, :]).astype(BF16), w1_ref[...]))
    ha = _dot((h + xx * mix_ref[1:2, :]).astype(BF16), a1_ref[...])
    hg = _sigmoid(_dot((h + xx * mix_ref[2:3, :]).astype(BF16), g1_ref[...]))
    gate_ref[0] = _dot(hg.astype(BF16), g2_ref[...])
    for dd in range(2):
        sl = slice(dd * LORA_PAD, (dd + 1) * LORA_PAD)
        z = w0_ref[dd] + _dot(hw[:, sl].astype(BF16), w2_ref[dd])
        w_log = -_softplus(-z) - 0.5
        dec_ref[dd, 0] = jnp.exp(-jnp.exp(w_log))
        aa_ref[dd, 0] = _sigmoid(a0_ref[dd] + _dot(ha[:, sl].astype(BF16), a2_ref[dd]))


def _pad_lora(w_in, w_out):
    nd, d, r = w_in.shape
    wi = jnp.zeros((d, nd * LORA_PAD), BF16)
    wo = jnp.zeros((nd, LORA_PAD, d), BF16)
    for dd in range(nd):
        wi = wi.at[:, dd * LORA_PAD:dd * LORA_PAD + r].set(w_in[dd].astype(BF16))
        wo = wo.at[dd, :r].set(w_out[dd].astype(BF16))
    return wi, wo


def _lora(h, mix3, w1p, w2p, a1p, a2p, g1, g2, w0, a0, n_lat_tiles):
    bn, tt, d = h.shape
    const2 = lambda s: pl.BlockSpec(s, lambda b, t: (0, 0))
    const3 = lambda s: pl.BlockSpec(s, lambda b, t: (0, 0, 0))
    kern = functools.partial(_lora_kernel, n_lat_tiles=n_lat_tiles * (TM // TL))
    dir_out = pl.BlockSpec((2, 1, TL, d), lambda b, t: (0, b, t, 0))
    tok_out = pl.BlockSpec((1, TL, d), lambda b, t: (b, t, 0))
    dir_shape = jax.ShapeDtypeStruct((2, bn, tt, d), F32)
    tok_shape = jax.ShapeDtypeStruct((bn, tt, d), F32)
    return pl.pallas_call(
        kern,
        grid=(bn, tt // TL),
        in_specs=_halo_specs(d, tt, TL) + [
            const2((3, d)), const2(w1p.shape), const2(a1p.shape), const2((d, g1.shape[1])),
            const3(w2p.shape), const3(a2p.shape), const2(g2.shape),
            const3((2, 1, d)), const3((2, 1, d)),
        ],
        out_specs=[dir_out, dir_out, tok_out],
        out_shape=[dir_shape, dir_shape, tok_shape],
        compiler_params=_cparams("arbitrary", "arbitrary"),
        name="rwkv_lora",
    )(h, h, h, mix3, w1p, a1p, g1, w2p, a2p, g2, w0.reshape(2, 1, d), a0.reshape(2, 1, d))


N_CH = RWKV_HEAD
N_RH = 32
N_GRP = LANES // N_RH


def _colperm_kernel(w_ref, p_ref, o_ref):
    o_ref[...] = _dot(w_ref[...].astype(BF16), p_ref[...]).astype(BF16)


def _colperm(w, p):
    r, d = w.shape
    tr = min(r, 512)
    return pl.pallas_call(
        _colperm_kernel,
        grid=(r // tr,),
        in_specs=[pl.BlockSpec((tr, d), lambda i: (i, 0)), pl.BlockSpec((d, d), lambda i: (0, 0))],
        out_specs=pl.BlockSpec((tr, d), lambda i: (i, 0)),
        out_shape=jax.ShapeDtypeStruct((r, d), BF16),
        compiler_params=_cparams("arbitrary"),
        name="weight_colperm",
    )(w, p)


def _rowperm_kernel(pt_ref, w_ref, o_ref):
    o_ref[...] = _dot(pt_ref[...], w_ref[...].astype(BF16)).astype(BF16)


def _rowperm(w, pt):
    d, n = w.shape
    tn = min(n, 512)
    return pl.pallas_call(
        _rowperm_kernel,
        grid=(n // tn,),
        in_specs=[pl.BlockSpec((d, d), lambda i: (0, 0)), pl.BlockSpec((d, tn), lambda i: (0, i))],
        out_specs=pl.BlockSpec((d, tn), lambda i: (0, i)),
        out_shape=jax.ShapeDtypeStruct((d, n), BF16),
        compiler_params=_cparams("arbitrary"),
        name="weight_rowperm",
    )(pt, w)


def _lane_group(rows):
    return lax.broadcasted_iota(jnp.int32, (rows, LANES), 1) // N_RH


def _pick_groups(pieces, lg):
    out = pieces[N_GRP - 1]
    for g in range(N_GRP - 2, -1, -1):
        out = jnp.where(lg == g, pieces[g], out)
    return out


def _to_chain_blocks(srcs, j, lg):
    blks = [s[:, j * LANES:(j + 1) * LANES] for s in srcs]
    rolled = []
    for sh in range(N_GRP):
        merged = _pick_groups([blks[(i + sh) % N_GRP] for i in range(N_GRP)], lg)
        rolled.append(merged if sh == 0 else pltpu.roll(merged, sh * N_RH, 1))
    return [_pick_groups([rolled[(g - i) % N_GRP] for g in range(N_GRP)], lg) for i in range(N_GRP)]


def _head_sum(x):
    s = x[:, 0:LANES]
    for j in range(1, x.shape[1] // LANES):
        s = s + x[:, j * LANES:(j + 1) * LANES]
    s = s + pltpu.roll(s, 2 * N_RH, 1)
    return s + pltpu.roll(s, N_RH, 1)


def _transpose8(xs):
    sub = lax.broadcasted_iota(jnp.int32, (HALO, LANES), 0)
    xs = list(xs)
    for s in (1, 2, 4):
        low = (sub & s) == 0
        for c0 in range(HALO):
            if c0 & s:
                continue
            a, b = xs[c0], xs[c0 + s]
            xs[c0] = jnp.where(low, a, pltpu.roll(b, s, 0))
            xs[c0 + s] = jnp.where(low, pltpu.roll(a, HALO - s, 0), b)
    return xs


def _store_chain_octet(ref, c0, blks):
    rows = blks[0].shape[0]
    for a in range(rows // HALO):
        tiles = _transpose8([blk[a * HALO:(a + 1) * HALO, :] for blk in blks])
        for t in range(HALO):
            ref[a * HALO + t, c0:c0 + HALO, :] = tiles[t]


def _load_chain_octet(ref, c0):
    rows = ref.shape[0]
    per_a = [_transpose8([ref[a * HALO + t, c0:c0 + HALO, :] for t in range(HALO)]) for a in range(rows // HALO)]
    return [jnp.concatenate([per_a[a][u] for a in range(rows // HALO)], axis=0) for u in range(HALO)]


def _prep_kernel(rkv_ref, dec_ref, aa_ref, kk_ref, ka_ref, rk_ref,
                 w_ref, kd_ref, bb_ref, nk_ref, v_ref, r_ref, bon_ref):
    rows = rkv_ref.shape[2]
    bn = rkv_ref.shape[1]
    lg = _lane_group(rows)
    nblk = kk_ref.shape[1] // LANES
    ks, kks = [], []
    for b in range(bn):
        k = rkv_ref[1, b]
        kk = k * kk_ref[...]
        inv = lax.rsqrt(jnp.maximum(_head_sum(kk * kk), 1e-24))
        ks.append(k)
        kks.append(kk * jnp.concatenate([inv] * nblk, axis=1))
    groups = [(dd, b) for dd in range(2) for b in range(bn)]
    kd_src = [ks[b] * (1.0 + (aa_ref[dd, b] - 1.0) * ka_ref[...]) for dd, b in groups]
    bb_src = [kks[b] * aa_ref[dd, b] for dd, b in groups]
    w_src = [dec_ref[dd, b] for dd, b in groups]
    nk_src = [-kks[b] for _, b in groups]
    v_src = [rkv_ref[2, b] for _, b in groups]
    r_src = [rkv_ref[0, b] for _, b in groups]
    bon = [_head_sum(rkv_ref[0, b] * rk_ref[...] * (kd_src[b] + kd_src[bn + b])) for b in range(bn)]
    bon_ref[...] = _pick_groups([bon[b] for _, b in groups], lg)
    per_oct = HALO // N_GRP
    for c8 in range(N_CH // HALO):
        for srcs, ref in ((w_src, w_ref), (kd_src, kd_ref), (bb_src, bb_ref), (nk_src, nk_ref),
                          (v_src, v_ref), (r_src, r_ref)):
            blks = []
            for j in range(c8 * per_oct, (c8 + 1) * per_oct):
                blks += _to_chain_blocks(srcs, j, lg)
            _store_chain_octet(ref, c8 * HALO, blks)


def _prep(rkv, dec, aa, k_k, k_a, r_k):
    _, bn, tt, d = rkv.shape
    assert bn * 2 == N_GRP and d == N_CH * N_RH
    out = pl.BlockSpec((TP, N_CH, LANES), lambda t: (t, 0, 0))
    shape = jax.ShapeDtypeStruct((tt, N_CH, LANES), F32)
    return pl.pallas_call(
        _prep_kernel,
        grid=(tt // TP,),
        in_specs=[
            pl.BlockSpec((3, bn, TP, d), lambda t: (0, 0, t, 0)),
            pl.BlockSpec((2, bn, TP, d), lambda t: (0, 0, t, 0)),
            pl.BlockSpec((2, bn, TP, d), lambda t: (0, 0, t, 0)),
            pl.BlockSpec((1, d), lambda t: (0, 0)),
            pl.BlockSpec((1, d), lambda t: (0, 0)),
            pl.BlockSpec((1, d), lambda t: (0, 0)),
        ],
        out_specs=[out] * 6 + [pl.BlockSpec((TP, LANES), lambda t: (t, 0))],
        out_shape=[shape] * 6 + [jax.ShapeDtypeStruct((tt, LANES), F32)],
        compiler_params=_cparams("arbitrary"),
        name="rwkv_prep",
    )(rkv, dec, aa, k_k.reshape(1, d), k_a.reshape(1, d), r_k.reshape(1, d))


def _scan_kernel(*refs, steps):
    fwd_refs, rev_refs = refs[0:6], refs[6:12]
    yf_ref, yr_ref, s_sc, op_sc, sa_sc, g_sc = refs[12:18]
    KT, BT, AT, V, RT = range(5)
    n_oct = N_CH // HALO
    assert steps % 2 == 0

    @pl.when(pl.program_id(0) == 0)
    def _():
        s_sc[...] = jnp.zeros(s_sc.shape, F32)
        op_sc[...] = jnp.zeros(op_sc.shape, F32)
        sa_sc[...] = jnp.zeros(sa_sc.shape, F32)
        g_sc[...] = jnp.ones(g_sc.shape, F32)

    is_fwd = lax.broadcasted_iota(jnp.int32, (1, LANES), 1) < LANES // 2
    octet = lambda j: slice(j * HALO, (j + 1) * HALO)

    def step(i, carry):
        ir = steps - 1 - i
        cur = i % 2
        prv = 1 - cur
        w, k, b, a, v, r = (jnp.where(is_fwd, f[i], g[ir]) for f, g in zip(fwd_refs, rev_refs))
        g_prev = g_sc[...]
        g_cur = g_prev * w
        g_inv = 1.0 / g_cur
        g_sc[...] = g_cur
        op_sc[cur, KT] = k * g_inv
        op_sc[cur, BT] = b * g_inv
        op_sc[cur, AT] = g_prev * a
        op_sc[cur, V] = v
        op_sc[cur, RT] = g_cur * r
        br = jnp.sum(b * r, axis=0, keepdims=True)
        kr = jnp.sum(k * r, axis=0, keepdims=True)
        sa_p = [sa_sc[octet(j), :] for j in range(n_oct)]
        vv_p = [op_sc[prv, V, octet(j), :] for j in range(n_oct)]

        def sweep(g, acc):
            sa, y0 = list(acc[:n_oct]), list(acc[n_oct:])
            for u in range(SCAN_UNROLL):
                kk = g * SCAN_UNROLL + u
                b_row = op_sc[prv, BT, pl.ds(kk, 1), :]
                k_row = op_sc[prv, KT, pl.ds(kk, 1), :]
                a_row = op_sc[cur, AT, pl.ds(kk, 1), :]
                r_row = op_sc[cur, RT, pl.ds(kk, 1), :]
                for j in range(n_oct):
                    s = s_sc[kk, octet(j), :] + sa_p[j] * b_row + vv_p[j] * k_row
                    s_sc[kk, octet(j), :] = s
                    sa[j] = sa[j] + s * a_row
                    y0[j] = y0[j] + s * r_row
            return tuple(sa) + tuple(y0)

        zero = jnp.zeros((HALO, LANES), F32)
        acc = lax.fori_loop(0, N_CH // SCAN_UNROLL, sweep, (zero,) * (2 * n_oct))
        for j in range(n_oct):
            sa_j, y0_j = acc[j], acc[n_oct + j]
            sa_sc[octet(j), :] = sa_j
            y = y0_j + sa_j * br + op_sc[cur, V, octet(j), :] * kr
            yf_ref[i, octet(j), :] = y
            yr_ref[ir, octet(j), :] = y
        return carry

    lax.fori_loop(0, steps, step, 0)

    last = (steps - 1) % 2
    sa_p = [sa_sc[octet(j), :] for j in range(n_oct)]
    vv_p = [op_sc[last, V, octet(j), :] for j in range(n_oct)]

    def flush(g, carry):
        for u in range(SCAN_UNROLL):
            kk = g * SCAN_UNROLL + u
            b_row = op_sc[last, BT, pl.ds(kk, 1), :]
            k_row = op_sc[last, KT, pl.ds(kk, 1), :]
            g_row = g_sc[pl.ds(kk, 1), :]
            for j in range(n_oct):
                s_sc[kk, octet(j), :] = (s_sc[kk, octet(j), :] + sa_p[j] * b_row + vv_p[j] * k_row) * g_row
        return carry

    lax.fori_loop(0, N_CH // SCAN_UNROLL, flush, 0)
    g_sc[...] = jnp.ones(g_sc.shape, F32)
    op_sc[last, KT] = jnp.zeros((N_CH, LANES), F32)
    op_sc[last, BT] = jnp.zeros((N_CH, LANES), F32)


def _scan(ops, lat_len, ctx_len, steps=32):
    tt = ops[0].shape[0]
    n_lat, n_ctx = lat_len // steps, ctx_len // steps

    def fwd_blk(i):
        return jnp.where(i < n_ctx, n_lat + i, i - n_ctx)

    def rev_blk(i):
        return jnp.where(i < n_ctx, n_lat + n_ctx - 1 - i, n_lat - 1 - (i - n_ctx))

    fwd = pl.BlockSpec((steps, N_CH, LANES), lambda i: (fwd_blk(i), 0, 0))
    rev = pl.BlockSpec((steps, N_CH, LANES), lambda i: (rev_blk(i), 0, 0))
    shape = jax.ShapeDtypeStruct((tt, N_CH, LANES), F32)
    return pl.pallas_call(
        functools.partial(_scan_kernel, steps=steps),
        grid=(tt // steps,),
        in_specs=[fwd] * 6 + [rev] * 6,
        out_specs=[fwd, rev],
        out_shape=[shape, shape],
        scratch_shapes=[pltpu.VMEM((N_CH, N_CH, LANES), F32), pltpu.VMEM((2, 5, N_CH, LANES), F32),
                        pltpu.VMEM((N_CH, LANES), F32), pltpu.VMEM((N_CH, LANES), F32)],
        compiler_params=_cparams("arbitrary"),
        name="rwkv_scan",
    )(*ops, *ops)


def _readout_kernel(yf_ref, yr_ref, v_ref, bon_ref, gate_ref, lnw_ref, lnb_ref, wo_ref,
                    x_ref, mod_ref, g_ref, xo_ref, f_ref, y_sc, o_sc):
    rows = yf_ref.shape[0]
    bn = x_ref.shape[0]
    half = LANES // 2
    lg = _lane_group(rows)
    blk = lambda ref, c: ref[:, c * LANES:(c + 1) * LANES]
    inv_n = 1.0 / N_CH
    mu = None
    for c8 in range(N_CH // HALO):
        yfs = _load_chain_octet(yf_ref, c8 * HALO)
        yrs = _load_chain_octet(yr_ref, c8 * HALO)
        for u in range(HALO):
            y = yfs[u] + pltpu.roll(yrs[u], half, 1)
            y_sc[:, (c8 * HALO + u) * LANES:(c8 * HALO + u + 1) * LANES] = y
            mu = y if mu is None else mu + y
    mu = mu * inv_n
    var = None
    for c in range(N_CH):
        dy = blk(y_sc, c) - mu
        var = dy * dy if var is None else var + dy * dy
    rstd = lax.rsqrt(var * inv_n + GN_EPS)
    bon = bon_ref[...]
    for c8 in range(N_CH // HALO):
        vs = _load_chain_octet(v_ref, c8 * HALO)
        for u in range(HALO):
            c = c8 * HALO + u
            o_sc[:, c * LANES:(c + 1) * LANES] = ((blk(y_sc, c) - mu) * rstd * blk(lnw_ref, c) + blk(lnb_ref, c)
                                                  + bon * vs[u])
    for b in range(bn):
        cols = []
        for j in range(N_CH // N_GRP):
            pieces = []
            for i in range(N_GRP):
                c = j * N_GRP + i
                shift = ((i - b) % N_GRP) * N_RH
                piece = o_sc[:, c * LANES:(c + 1) * LANES]
                pieces.append(piece if shift == 0 else pltpu.roll(piece, shift, 1))
            cols.append(_pick_groups(pieces, lg))
        o = jnp.concatenate(cols, axis=1) * gate_ref[b]
        yo = _dot(o.astype(BF16), wo_ref[...])
        xn = x_ref[b] + mod_ref[b, 2:3, :] * _rms(yo, g_ref[1:2, :])
        xo_ref[b] = xn
        f_ref[b] = _rms(xn, g_ref[2:3, :]) * (1.0 + mod_ref[b, 4:5, :]) + mod_ref[b, 3:4, :]


def _chain_param(p):
    pc = p.reshape(N_RH, N_CH).T
    return jnp.tile(pc, (1, N_GRP)).reshape(1, N_CH * LANES)


def _readout(yf, yr, v_c, bon, gate, ln_w, ln_b, wo_bf, x_all, mods, g4, lat_len):
    bn, tt, d = x_all.shape
    cw = N_CH * LANES
    chain = pl.BlockSpec((TP, N_CH, LANES), lambda t: (t, 0, 0))
    tok = pl.BlockSpec((bn, TP, d), lambda t: (0, t, 0))
    row = pl.BlockSpec((1, cw), lambda t: (0, 0))
    out_shape = jax.ShapeDtypeStruct((bn, lat_len, d), F32)
    return pl.pallas_call(
        _readout_kernel,
        grid=(lat_len // TP,),
        in_specs=[
            chain, chain, chain, pl.BlockSpec((TP, LANES), lambda t: (t, 0)), tok, row, row,
            pl.BlockSpec((d, d), lambda t: (0, 0)),
            tok,
            pl.BlockSpec((bn + 1, 6, d), lambda t: (0, 0, 0)),
            pl.BlockSpec((4, d), lambda t: (0, 0)),
        ],
        out_specs=[tok, tok],
        out_shape=[out_shape, out_shape],
        scratch_shapes=[pltpu.VMEM((TP, cw), F32), pltpu.VMEM((TP, cw), F32)],
        compiler_params=_cparams("arbitrary"),
        name="rwkv_readout",
    )(yf, yr, v_c, bon, gate, _chain_param(ln_w), _chain_param(ln_b), wo_bf, x_all, mods, g4)


def _rope_tables(lat_len, ctx_len):
    m = HEAD_DIM // 4
    t = jnp.arange(lat_len)
    inv = ROPE_THETA ** (-jnp.arange(0, 2 * m, 2, dtype=F32) / (2 * m))
    ar = (t // GRID_W).astype(F32)[:, None] * inv[None, :]
    ac = (t % GRID_W).astype(F32)[:, None] * inv[None, :]
    cos = jnp.concatenate([jnp.cos(ar), jnp.cos(ar), jnp.cos(ac), jnp.cos(ac)], axis=-1)
    sin = jnp.concatenate([-jnp.sin(ar), jnp.sin(ar), -jnp.sin(ac), jnp.sin(ac)], axis=-1)
    cos = jnp.concatenate([cos, jnp.ones((ctx_len, HEAD_DIM), F32)], axis=0)
    sin = jnp.concatenate([sin, jnp.zeros((ctx_len, HEAD_DIM), F32)], axis=0)
    return cos, sin


def kernel(x, c, ctx, c_ctx, w_mod, b_mod, norm_g, ab_w_in, ab_q_gain, ab_k_gain, pool_w, pool_scale, ab_w_out, rw_mix, rw_wr, rw_wk, rw_wv, rw_wo, rw_w0, rw_w1, rw_w2, rw_a0, rw_a1, rw_a2, rw_g1, rw_g2, rw_k_k, rw_k_a, rw_r_k, rw_ln_w, rw_ln_b, router_w, router_b, moe_w_gate, moe_b_gate, moe_w_up, moe_b_up, moe_w_down, moe_b_down):
    bn, lat_len, d = x.shape
    ctx_len = ctx.shape[1]
    assert lat_len % TM == 0 and ctx_len == TM and d % LANES == 0
    n_lat_tiles = lat_len // TM
    tt = lat_len + ctx_len
    nt = tt // TM

    cvec = jnp.zeros((HALO, d), F32).at[:bn].set(c).at[bn].set(c_ctx)
    mods_all = _mod_rows(cvec, w_mod, b_mod)
    mods = [mods_all[i, :bn + 1].reshape(bn + 1, 6, d) for i in range(w_mod.shape[0])]

    x_all = jnp.concatenate([x, ctx], axis=1)

    cos_t, sin_t = _rope_tables(lat_len, ctx_len)
    q, k, v, pin = _inproj(x_all, mods[0], norm_g[0], ab_w_in[0].astype(BF16), ab_q_gain[0], ab_k_gain[0],
                           cos_t, sin_t, n_lat_tiles)
    o = _attention(q, k, v, n_lat_tiles)
    x_all, f = _mix0_out(o, pin, pool_w[0].astype(BF16), pool_scale[0], ab_w_out[0].astype(BF16), x_all,
                         mods[0], norm_g[0], n_lat_tiles, lat_len, ctx_len)
    tile_mod = jnp.where(jnp.arange(bn * nt) % nt < n_lat_tiles, jnp.arange(bn * nt) // nt, bn).astype(jnp.int32)
    x_all = _moe(f.reshape(bn * tt, d), x_all.reshape(bn * tt, d), tile_mod, mods[0], norm_g[0],
                 router_w[0], router_b[0], 0, moe_w_gate, moe_b_gate[0], moe_w_up, moe_b_up[0],
                 moe_w_down, moe_b_down[0]).reshape(bn, tt, d)

    h = _normmod(x_all, mods[1], norm_g[1], n_lat_tiles)
    mix = rw_mix[0]
    perm = jnp.arange(d).reshape(N_RH, N_CH).T.reshape(-1)
    pmat = (jnp.arange(d)[:, None] == perm[None, :]).astype(BF16)
    w3 = _colperm(jnp.concatenate([rw_wr[0], rw_wk[0], rw_wv[0]], axis=0), pmat).reshape(3, d, d)
    rkv = _shiftproj(h, jnp.stack([mix[0], mix[2], mix[3]])[:, None, :], w3, n_lat_tiles)
    w1p, w2p = _pad_lora(rw_w1[0], rw_w2[0])
    a1p, a2p = _pad_lora(rw_a1[0], rw_a2[0])
    w2p = _colperm(w2p.reshape(2 * LORA_PAD, d), pmat).reshape(2, LORA_PAD, d)
    a2p = _colperm(a2p.reshape(2 * LORA_PAD, d), pmat).reshape(2, LORA_PAD, d)
    dec, aa, gate = _lora(h, jnp.stack([mix[1], mix[4], mix[5]]), w1p, w2p, a1p, a2p, rw_g1[0].astype(BF16),
                          _colperm(rw_g2[0], pmat), rw_w0[0][:, perm], rw_a0[0][:, perm], n_lat_tiles)
    w_c, kd_c, bb_c, nk_c, v_c, r_c, bon = _prep(rkv, dec, aa, rw_k_k[0][perm], rw_k_a[0][perm],
                                                 rw_r_k[0].reshape(-1)[perm])
    yf, yr = _scan([w_c, kd_c, bb_c, nk_c, v_c, r_c], lat_len, ctx_len)
    x_lat, f = _readout(yf, yr, v_c, bon, gate, rw_ln_w[0], rw_ln_b[0],
                        _rowperm(rw_wo[0], (perm[:, None] == jnp.arange(d)[None, :]).astype(BF16)), x_all,
                        mods[1], norm_g[1], lat_len)
    tile_mod = (jnp.arange(bn * n_lat_tiles) // n_lat_tiles).astype(jnp.int32)
    out = _moe(f.reshape(bn * lat_len, d), x_lat.reshape(bn * lat_len, d), tile_mod, mods[1], norm_g[1],
               router_w[1], router_b[1], 1, moe_w_gate, moe_b_gate[1], moe_w_up, moe_b_up[1],
               moe_w_down, moe_b_down[1])
    return out.reshape(bn, lat_len, d)
```
